```python
import jax, jax.numpy as jnp
from jax import lax
import numpy as np

D_MODEL = 1024
BATCH = 8
SEQ = 2048
DEPTH = 2
DEC_BATCH = 128
DEC_SEQ = 4
PAST_LEN = 16384
PAGE_SIZE = 128

EXPAND = 2
D_MIX = EXPAND * D_MODEL
D_A = D_MIX // 2
D_B = D_MIX - D_A
N_HEADS_A = 8
HEAD_DIM_A = D_A // N_HEADS_A
N_HEADS_B = 8
CONV_A_W = 31
CONV_B_W = 3
D_IN = 3 * D_A + 4 * D_B
RMS_EPS = 1e-6
LN_EPS = 1e-5

kernel_name = "hybrid_conformer_shortconv_decode_step"


def rmsnorm(x, g):
    xf = x.astype(jnp.float32)
    xn = xf * lax.rsqrt(jnp.mean(xf * xf, axis=-1, keepdims=True) + RMS_EPS)
    return xn.astype(x.dtype) * g


def head_layernorm(x, g, b):
    bsz, t, c = x.shape
    xf = x.astype(jnp.float32).reshape(bsz, t, N_HEADS_A, HEAD_DIM_A)
    mu = jnp.mean(xf, axis=-1, keepdims=True)
    var = jnp.mean(jnp.square(xf - mu), axis=-1, keepdims=True)
    xn = ((xf - mu) * lax.rsqrt(var + LN_EPS)).reshape(bsz, t, c)
    return xn.astype(x.dtype) * g + b


def causal_dwconv(x, buf, w):
    width, c = w.shape
    xp = jnp.concatenate([buf.astype(x.dtype), x], axis=1)
    y = lax.conv_general_dilated(
        xp, w.reshape(width, 1, c).astype(x.dtype),
        window_strides=(1,), padding='VALID',
        dimension_numbers=('NWC', 'WIO', 'NWC'),
        feature_group_count=c)
    new_buf = xp[:, xp.shape[1] - (width - 1):, :]
    return y, new_buf


def hybrid_layer(x, buf_a, buf_b, norm_g, w_in, conv_a_w, conv_a_b, ln_a_g, ln_a_b, conv_b_w, w_out):
    h = rmsnorm(x, norm_g)
    p = jnp.einsum('btd,de->bte', h, w_in)
    splits = [D_A, 2 * D_A, 3 * D_A, 3 * D_A + D_B, 3 * D_A + 2 * D_B, 3 * D_A + 3 * D_B]
    a_val, a_gate, z_a, gb, gc, hb, z_b = jnp.split(p, splits, axis=-1)
    u = a_val * jax.nn.sigmoid(a_gate)
    ca, new_a = causal_dwconv(u, buf_a, conv_a_w)
    ca = head_layernorm(ca + conv_a_b, ln_a_g, ln_a_b)
    y_a = jax.nn.silu(ca) * jax.nn.silu(z_a)
    v = gc * hb
    cb, new_b = causal_dwconv(v, buf_b, conv_b_w)
    y_b = gb * cb * jax.nn.silu(z_b)
    y = jnp.einsum('bte,ed->btd', jnp.concatenate([y_a, y_b], axis=-1), w_out)
    return x + y, new_a, new_b


def trunk(x, bufs_a, bufs_b, norm_g, w_in, conv_a_w, conv_a_b, ln_a_g, ln_a_b, conv_b_w, w_out, final_g):
    new_as, new_bs = [], []
    for l in range(DEPTH):
        x, na, nb = hybrid_layer(x, bufs_a[l], bufs_b[l], norm_g[l], w_in[l], conv_a_w[l], conv_a_b[l],
                                 ln_a_g[l], ln_a_b[l], conv_b_w[l], w_out[l])
        new_as.append(na)
        new_bs.append(nb)
    return rmsnorm(x, final_g), jnp.stack(new_as, axis=0), jnp.stack(new_bs, axis=0)


def setup_inputs(seed: int = 0) -> dict:
    key = jax.random.key(seed)
    ks = jax.random.split(key, 14)
    f32 = jnp.float32
    x_prompt = jax.random.normal(ks[0], (BATCH, SEQ, D_MODEL), f32)
    x_sample = jax.random.normal(ks[1], (DEC_BATCH, DEC_SEQ, D_MODEL), f32)
    state_conv_a = 0.5 * jax.random.normal(ks[2], (DEPTH, DEC_BATCH, CONV_A_W - 1, D_A), f32)
    state_conv_b = 0.5 * jax.random.normal(ks[3], (DEPTH, DEC_BATCH, CONV_B_W - 1, D_B), f32)
    norm_g = 1.0 + 0.02 * jax.random.normal(ks[4], (DEPTH, D_MODEL), f32)
    w_in = jax.random.normal(ks[5], (DEPTH, D_MODEL, D_IN), f32) * D_MODEL ** -0.5
    conv_a_w = jax.random.normal(ks[6], (DEPTH, CONV_A_W, D_A), f32) * CONV_A_W ** -0.5
    conv_a_b = 0.01 * jax.random.normal(ks[7], (DEPTH, D_A), f32)
    ln_a_g = 1.0 + 0.02 * jax.random.normal(ks[8], (DEPTH, D_A), f32)
    ln_a_b = 0.01 * jax.random.normal(ks[9], (DEPTH, D_A), f32)
    conv_b_w = jax.random.normal(ks[10], (DEPTH, CONV_B_W, D_B), f32) * CONV_B_W ** -0.5
    w_out = jax.random.normal(ks[11], (DEPTH, D_MIX, D_MODEL), f32) * D_MIX ** -0.5
    final_g = 1.0 + 0.02 * jax.random.normal(ks[12], (D_MODEL,), f32)
    return {"x_prompt": x_prompt, "x_sample": x_sample,
            "state_conv_a": state_conv_a, "state_conv_b": state_conv_b,
            "norm_g": norm_g, "w_in": w_in, "conv_a_w": conv_a_w, "conv_a_b": conv_a_b,
            "ln_a_g": ln_a_g, "ln_a_b": ln_a_b, "conv_b_w": conv_b_w, "w_out": w_out,
            "final_g": final_g}


def reference(x_prompt, x_sample, state_conv_a, state_conv_b, norm_g, w_in, conv_a_w, conv_a_b,
              ln_a_g, ln_a_b, conv_b_w, w_out, final_g):
    bp = x_prompt.shape[0]
    zeros_a = jnp.zeros((DEPTH, bp, CONV_A_W - 1, D_A), x_prompt.dtype)
    zeros_b = jnp.zeros((DEPTH, bp, CONV_B_W - 1, D_B), x_prompt.dtype)
    y_prompt, new_conv_a_prompt, new_conv_b_prompt = trunk(
        x_prompt, zeros_a, zeros_b, norm_g, w_in, conv_a_w, conv_a_b, ln_a_g, ln_a_b, conv_b_w, w_out, final_g)
    y_sample, new_conv_a_sample, new_conv_b_sample = trunk(
        x_sample, state_conv_a, state_conv_b, norm_g, w_in, conv_a_w, conv_a_b, ln_a_g, ln_a_b, conv_b_w, w_out, final_g)
    return (y_prompt, y_sample, new_conv_a_prompt, new_conv_b_prompt, new_conv_a_sample, new_conv_b_sample)
```

```python
import functools

import jax
import jax.numpy as jnp
from jax import lax
from jax.experimental import pallas as pl
from jax.experimental.pallas import tpu as pltpu

F32 = jnp.float32
BF16 = jnp.bfloat16

HEAD_DIM_A = 128
RMS_EPS = 1e-6
LN_EPS = 1e-5

SUBLANES = 8
MXU_COLS = 256
HALO_A = 32
HALO_B = 8
ROW_GROUP = 16
PROMPT_TILE = 256
SAMPLE_SEQS = 32
VMEM_LIMIT_BYTES = 56 * 1024 * 1024


def _sigmoid(x):
    return 1.0 / (1.0 + jnp.exp(-x))


def _silu(x):
    return x * _sigmoid(x)


def _rms_scale(x):
    return x * lax.rsqrt(jnp.mean(x * x, axis=-1, keepdims=True) + RMS_EPS)


def _head_layernorm(ca, g, b):
    outs = []
    for j in range(ca.shape[-1] // HEAD_DIM_A):
        c = ca[:, j * HEAD_DIM_A:(j + 1) * HEAD_DIM_A]
        mu = jnp.mean(c, axis=-1, keepdims=True)
        d = c - mu
        var = jnp.mean(d * d, axis=-1, keepdims=True)
        outs.append(d * lax.rsqrt(var + LN_EPS))
    return jnp.concatenate(outs, axis=-1) * g + b


def _prompt_layer_kernel(x_ref, ng_ref, win_ref, caw_ref, cab_ref, lng_ref, lnb_ref, cbw_ref,
                         wout_ref, fg_ref, y_ref, na_ref, nb_ref,
                         h_s, u_s, v_s, y_s, *, apply_final):
    tm, d_model = x_ref.shape
    conv_a_w, d_a = caw_ref.shape
    conv_b_w, d_b = cbw_ref.shape
    off_g, off_za = d_a, 2 * d_a
    off_gb, off_gc, off_hb, off_zb = 3 * d_a, 3 * d_a + d_b, 3 * d_a + 2 * d_b, 3 * d_a + 3 * d_b
    hist_a = conv_a_w - 1
    hist_b = conv_b_w - 1

    n_heads = d_a // HEAD_DIM_A

    def head_slabs(c0, width):
        return [(hc // HEAD_DIM_A, slice(hc, hc + HEAD_DIM_A), slice(hc - c0, hc - c0 + HEAD_DIM_A))
                for hc in range(c0, c0 + width, HEAD_DIM_A)]

    @pl.when(pl.program_id(1) == 0)
    def _():
        u_s[:, 0:HALO_A, :] = jnp.zeros((n_heads, HALO_A, HEAD_DIM_A), F32)
        v_s[:, 0:HALO_B, :] = jnp.zeros((n_heads, HALO_B, HEAD_DIM_A), F32)

    x = x_ref[...]
    h_s[...] = (_rms_scale(x) * ng_ref[...]).astype(BF16)

    def proj(off):
        return jnp.dot(h_s[...], win_ref[:, off:off + MXU_COLS], preferred_element_type=F32)

    for c0 in range(0, d_a, MXU_COLS):
        u = proj(c0) * _sigmoid(proj(off_g + c0))
        za = proj(off_za + c0)
        for hd, hcols, lcols in head_slabs(c0, MXU_COLS):
            u_s[hd, HALO_A:HALO_A + tm, :] = u[:, lcols]
            taps = [jnp.broadcast_to(caw_ref[k:k + 1, hcols], (SUBLANES, HEAD_DIM_A))
                    for k in range(conv_a_w)]
            for r in range(0, tm, ROW_GROUP):
                accs = []
                for s in range(r, r + ROW_GROUP, SUBLANES):
                    acc = None
                    for k in range(conv_a_w):
                        start = s + HALO_A - hist_a + k
                        term = u_s[hd, start:start + SUBLANES, :] * taps[k]
                        acc = term if acc is None else acc + term
                    accs.append(acc)
                ca = jnp.concatenate(accs, axis=0) + cab_ref[:, hcols]
                ca = _head_layernorm(ca, lng_ref[:, hcols], lnb_ref[:, hcols])
                y_s[r:r + ROW_GROUP, hcols] = (
                    _silu(ca) * _silu(za[r:r + ROW_GROUP, lcols])).astype(BF16)
        v = proj(off_gc + c0) * proj(off_hb + c0)
        cbs = []
        for hd, hcols, lcols in head_slabs(c0, MXU_COLS):
            v_s[hd, HALO_B:HALO_B + tm, :] = v[:, lcols]
            cb = None
            for k in range(conv_b_w):
                start = HALO_B - hist_b + k
                term = v_s[hd, start:start + tm, :] * cbw_ref[k:k + 1, hcols]
                cb = term if cb is None else cb + term
            cbs.append(cb)
        y_s[:, d_a + c0:d_a + c0 + MXU_COLS] = (
            proj(off_gb + c0) * jnp.concatenate(cbs, axis=-1) * _silu(proj(off_zb + c0))).astype(BF16)

    for hd, hcols, _ in head_slabs(0, d_a):
        na_ref[:, hcols] = u_s[hd, HALO_A + tm - hist_a:HALO_A + tm, :]
        nb_ref[:, hcols] = v_s[hd, HALO_B + tm - hist_b:HALO_B + tm, :]
    u_s[:, 0:HALO_A, :] = u_s[:, tm:tm + HALO_A, :]
    v_s[:, 0:HALO_B, :] = v_s[:, tm:tm + HALO_B, :]

    out = x + jnp.dot(y_s[...], wout_ref[...], preferred_element_type=F32)
    if apply_final:
        out = _rms_scale(out) * fg_ref[...]
    y_ref[...] = out


def _const_spec(block_shape, index):
    return pl.BlockSpec(block_shape, lambda *_: index, pipeline_mode=pl.Buffered(1))


def _prompt_layer(x, layer, norm_g, w_in, conv_a_w, conv_a_b, ln_a_g, ln_a_b, conv_b_w, w_out,
                  final_g, *, apply_final):
    bsz, seq, d_model = x.shape
    depth, _, d_in = w_in.shape
    conv_a_width, d_a = conv_a_w.shape[1:]
    conv_b_width, d_b = conv_b_w.shape[1:]
    d_mix = w_out.shape[1]
    tm = PROMPT_TILE
    assert seq % tm == 0 and tm % ROW_GROUP == 0
    assert d_a % MXU_COLS == 0 and d_a == d_b and d_mix == d_a + d_b
    assert conv_a_width - 1 <= HALO_A and conv_b_width - 1 <= HALO_B

    lay = lambda *tail: (layer,) + tail
    in_specs = [
        pl.BlockSpec((None, tm, d_model), lambda b, t: (b, t, 0)),
        _const_spec((None, 1, d_model), lay(0, 0)),
        _const_spec((None, d_model, d_in), lay(0, 0)),
        _const_spec((None, conv_a_width, d_a), lay(0, 0)),
        _const_spec((None, 1, d_a), lay(0, 0)),
        _const_spec((None, 1, d_a), lay(0, 0)),
        _const_spec((None, 1, d_a), lay(0, 0)),
        _const_spec((None, conv_b_width, d_b), lay(0, 0)),
        _const_spec((None, d_mix, d_model), lay(0, 0)),
        _const_spec((1, d_model), (0, 0)),
    ]
    out_specs = [
        pl.BlockSpec((None, tm, d_model), lambda b, t: (b, t, 0)),
        pl.BlockSpec((None, conv_a_width - 1, d_a), lambda b, t: (b, 0, 0)),
        pl.BlockSpec((None, conv_b_width - 1, d_b), lambda b, t: (b, 0, 0)),
    ]
    out_shape = [
        jax.ShapeDtypeStruct((bsz, seq, d_model), F32),
        jax.ShapeDtypeStruct((bsz, conv_a_width - 1, d_a), F32),
        jax.ShapeDtypeStruct((bsz, conv_b_width - 1, d_b), F32),
    ]
    scratch = [
        pltpu.VMEM((tm, d_model), BF16),
        pltpu.VMEM((d_a // HEAD_DIM_A, HALO_A + tm, HEAD_DIM_A), F32),
        pltpu.VMEM((d_b // HEAD_DIM_A, HALO_B + tm, HEAD_DIM_A), F32),
        pltpu.VMEM((tm, d_mix), BF16),
    ]
    return pl.pallas_call(
        functools.partial(_prompt_layer_kernel, apply_final=apply_final),
        grid=(bsz, seq // tm),
        in_specs=in_specs, out_specs=out_specs, out_shape=out_shape, scratch_shapes=scratch,
        compiler_params=pltpu.CompilerParams(
            dimension_semantics=("arbitrary", "arbitrary"), vmem_limit_bytes=VMEM_LIMIT_BYTES),
        name=f"prompt_layer{layer}",
    )(x, norm_g[:, None, :], w_in, conv_a_w, conv_a_b[:, None, :], ln_a_g[:, None, :],
      ln_a_b[:, None, :], conv_b_w, w_out, final_g[None, :])


def _sample_layer_kernel(x_ref, sa_ref, sb_ref, ng_ref, win_ref, caw_ref, cab_ref, lng_ref, lnb_ref,
                         cbw_ref, wout_ref, fg_ref, y_ref, na_ref, nb_ref, y_s, *, apply_final):
    sb, steps, d_model = x_ref.shape
    conv_a_w, d_a = caw_ref.shape
    conv_b_w, d_b = cbw_ref.shape
    off_g, off_za = d_a, 2 * d_a
    off_gb, off_gc, off_hb, off_zb = 3 * d_a, 3 * d_a + d_b, 3 * d_a + 2 * d_b, 3 * d_a + 3 * d_b
    hist_a = conv_a_w - 1
    hist_b = conv_b_w - 1

    x = jnp.concatenate([x_ref[:, t, :] for t in range(steps)], axis=0)
    h = (_rms_scale(x) * ng_ref[...]).astype(BF16)

    def proj(off):
        return jnp.dot(h, win_ref[:, off:off + MXU_COLS], preferred_element_type=F32)

    def step_rows(a, t):
        return a[t * sb:(t + 1) * sb]

    for c0 in range(0, d_a, MXU_COLS):
        cols = slice(c0, c0 + MXU_COLS)
        u = proj(c0) * _sigmoid(proj(off_g + c0))
        acc = [None] * steps
        for j in range(hist_a + steps):
            row = sa_ref[:, j, cols] if j < hist_a else step_rows(u, j - hist_a)
            for t in range(steps):
                k = j - t
                if 0 <= k < conv_a_w:
                    term = row * caw_ref[k:k + 1, cols]
                    acc[t] = term if acc[t] is None else acc[t] + term
        ca = jnp.concatenate(acc, axis=0) + cab_ref[:, cols]
        ca = _head_layernorm(ca, lng_ref[:, cols], lnb_ref[:, cols])
        y_s[:, cols] = (_silu(ca) * _silu(proj(off_za + c0))).astype(BF16)
        na_ref[:, 0:hist_a - steps, cols] = sa_ref[:, steps:hist_a, cols]
        for t in range(steps):
            na_ref[:, hist_a - steps + t, cols] = step_rows(u, t)

        v = proj(off_gc + c0) * proj(off_hb + c0)
        accb = [None] * steps
        for j in range(hist_b + steps):
            row = sb_ref[:, j, cols] if j < hist_b else step_rows(v, j - hist_b)
            for t in range(steps):
                k = j - t
                if 0 <= k < conv_b_w:
                    term = row * cbw_ref[k:k + 1, cols]
                    accb[t] = term if accb[t] is None else accb[t] + term
        cb = jnp.concatenate(accb, axis=0)
        y_s[:, d_a + c0:d_a + c0 + MXU_COLS] = (
            proj(off_gb + c0) * cb * _silu(proj(off_zb + c0))).astype(BF16)
        for j in range(hist_b):
            nb_ref[:, j, cols] = step_rows(v, steps - hist_b + j)

    out = x + jnp.dot(y_s[...], wout_ref[...], preferred_element_type=F32)
    if apply_final:
        out = _rms_scale(out) * fg_ref[...]
    for t in range(steps):
        y_ref[:, t, :] = step_rows(out, t)


def _sample_layer(x, layer, state_a, state_b, norm_g, w_in, conv_a_w, conv_a_b, ln_a_g, ln_a_b,
                  conv_b_w, w_out, final_g, *, apply_final):
    nseq, steps, d_model = x.shape
    d_in = w_in.shape[2]
    conv_a_width, d_a = conv_a_w.shape[1:]
    conv_b_width, d_b = conv_b_w.shape[1:]
    d_mix = w_out.shape[1]
    sb = SAMPLE_SEQS
    assert nseq % sb == 0 and sb % SUBLANES == 0
    assert steps <= conv_b_width - 1 + steps and steps <= conv_a_width - 1
    assert d_a % MXU_COLS == 0 and d_a == d_b and d_mix == d_a + d_b

    lay = lambda *tail: (layer,) + tail
    in_specs = [
        pl.BlockSpec((sb, steps, d_model), lambda i: (i, 0, 0)),
        pl.BlockSpec((None, sb, conv_a_width - 1, d_a), lambda i: (layer, i, 0, 0)),
        pl.BlockSpec((None, sb, conv_b_width - 1, d_b), lambda i: (layer, i, 0, 0)),
        _const_spec((None, 1, d_model), lay(0, 0)),
        _const_spec((None, d_model, d_in), lay(0, 0)),
        _const_spec((None, conv_a_width, d_a), lay(0, 0)),
        _const_spec((None, 1, d_a), lay(0, 0)),
        _const_spec((None, 1, d_a), lay(0, 0)),
        _const_spec((None, 1, d_a), lay(0, 0)),
        _const_spec((None, conv_b_width, d_b), lay(0, 0)),
        _const_spec((None, d_mix, d_model), lay(0, 0)),
        _const_spec((1, d_model), (0, 0)),
    ]
    out_specs = [
        pl.BlockSpec((sb, steps, d_model), lambda i: (i, 0, 0)),
        pl.BlockSpec((sb, conv_a_width - 1, d_a), lambda i: (i, 0, 0)),
        pl.BlockSpec((sb, conv_b_width - 1, d_b), lambda i: (i, 0, 0)),
    ]
    out_shape = [
        jax.ShapeDtypeStruct((nseq, steps, d_model), F32),
        jax.ShapeDtypeStruct((nseq, conv_a_width - 1, d_a), F32),
        jax.ShapeDtypeStruct((nseq, conv_b_width - 1, d_b), F32),
    ]
    return pl.pallas_call(
        functools.partial(_sample_layer_kernel, apply_final=apply_final),
        grid=(nseq // sb,),
        in_specs=in_specs, out_specs=out_specs, out_shape=out_shape,
        scratch_shapes=[pltpu.VMEM((sb * steps, d_mix), BF16)],
        compiler_params=pltpu.CompilerParams(
            dimension_semantics=("arbitrary",), vmem_limit_bytes=VMEM_LIMIT_BYTES),
        name=f"sample_layer{layer}",
    )(x, state_a, state_b, norm_g[:, None, :], w_in, conv_a_w, conv_a_b[:, None, :],
      ln_a_g[:, None, :], ln_a_b[:, None, :], conv_b_w, w_out, final_g[None, :])


def kernel(x_prompt, x_sample, state_conv_a, state_conv_b, norm_g, w_in, conv_a_w, conv_a_b,
           ln_a_g, ln_a_b, conv_b_w, w_out, final_g):
    depth = w_in.shape[0]
    w_in_bf = w_in.astype(BF16)
    w_out_bf = w_out.astype(BF16)
    weights = (norm_g, w_in_bf, conv_a_w, conv_a_b, ln_a_g, ln_a_b, conv_b_w, w_out_bf, final_g)

    xp, xs = x_prompt, x_sample
    pa, pb, sa, sb = [], [], [], []
    for layer in range(depth):
        last = layer == depth - 1
        xp, na, nb = _prompt_layer(xp, layer, *weights, apply_final=last)
        pa.append(na)
        pb.append(nb)
        xs, na, nb = _sample_layer(xs, layer, state_conv_a, state_conv_b, *weights, apply_final=last)
        sa.append(na)
        sb.append(nb)
    return (xp, xs, jnp.stack(pa), jnp.stack(pb), jnp.stack(sa), jnp.stack(sb))
```

```python
import functools

import jax
import jax.numpy as jnp
from jax import lax
from jax.experimental import pallas as pl
from jax.experimental.pallas import tpu as pltpu

F32 = jnp.float32
BF16 = jnp.bfloat16

HEAD_DIM_A = 128
RMS_EPS = 1e-6
LN_EPS = 1e-5

SUBLANES = 8
MXU_COLS = 256
HALO_A = 32
HALO_B = 8
ROW_GROUP = 16
PROMPT_TILE = 512
SAMPLE_SEQS = 32
VMEM_LIMIT_BYTES = 56 * 1024 * 1024


def _sigmoid(x):
    return 1.0 / (1.0 + jnp.exp(-x))


def _silu(x):
    return x * _sigmoid(x)


def _rms_scale(x):
    return x * lax.rsqrt(jnp.mean(x * x, axis=-1, keepdims=True) + RMS_EPS)


def _head_layernorm(ca, g, b):
    outs = []
    for j in range(ca.shape[-1] // HEAD_DIM_A):
        c = ca[:, j * HEAD_DIM_A:(j + 1) * HEAD_DIM_A]
        mu = jnp.mean(c, axis=-1, keepdims=True)
        d = c - mu
        var = jnp.mean(d * d, axis=-1, keepdims=True)
        outs.append(d * lax.rsqrt(var + LN_EPS))
    return jnp.concatenate(outs, axis=-1) * g + b


def _out_proj(y, wout_ref):
    return jnp.concatenate(
        [jnp.dot(y, wout_ref[n], preferred_element_type=F32) for n in range(wout_ref.shape[0])], axis=-1)


def _prompt_layer_kernel(x_ref, ng_ref, win_ref, caw_ref, cab_ref, lng_ref, lnb_ref, cbw_ref,
                         wout_ref, fg_ref, y_ref, na_ref, nb_ref,
                         h_s, u_s, v_s, y_s, *, apply_final):
    tm, d_model = x_ref.shape
    conv_a_w, d_a = caw_ref.shape
    conv_b_w, d_b = cbw_ref.shape
    off_g, off_za = d_a, 2 * d_a
    off_gb, off_gc, off_hb, off_zb = 3 * d_a, 3 * d_a + d_b, 3 * d_a + 2 * d_b, 3 * d_a + 3 * d_b
    hist_a = conv_a_w - 1
    hist_b = conv_b_w - 1

    n_heads = d_a // HEAD_DIM_A

    def head_slabs(c0, width):
        return [(hc // HEAD_DIM_A, slice(hc, hc + HEAD_DIM_A), slice(hc - c0, hc - c0 + HEAD_DIM_A))
                for hc in range(c0, c0 + width, HEAD_DIM_A)]

    @pl.when(pl.program_id(1) == 0)
    def _():
        u_s[:, 0:HALO_A, :] = jnp.zeros((n_heads, HALO_A, HEAD_DIM_A), F32)
        v_s[:, 0:HALO_B, :] = jnp.zeros((n_heads, HALO_B, HEAD_DIM_A), F32)

    x = x_ref[...]
    h_s[...] = (_rms_scale(x) * ng_ref[...]).astype(BF16)

    def proj(off):
        return jnp.dot(h_s[...], win_ref[off // MXU_COLS], preferred_element_type=F32)

    for c0 in range(0, d_a, MXU_COLS):
        u = proj(c0) * _sigmoid(proj(off_g + c0))
        za = proj(off_za + c0)
        for hd, hcols, lcols in head_slabs(c0, MXU_COLS):
            u_s[hd, HALO_A:HALO_A + tm, :] = u[:, lcols]
            taps = [jnp.broadcast_to(caw_ref[k:k + 1, hcols], (SUBLANES, HEAD_DIM_A))
                    for k in range(conv_a_w)]
            for r in range(0, tm, ROW_GROUP):
                accs = []
                for s in range(r, r + ROW_GROUP, SUBLANES):
                    acc = None
                    for k in range(conv_a_w):
                        start = s + HALO_A - hist_a + k
                        term = u_s[hd, start:start + SUBLANES, :] * taps[k]
                        acc = term if acc is None else acc + term
                    accs.append(acc)
                ca = jnp.concatenate(accs, axis=0) + cab_ref[:, hcols]
                ca = _head_layernorm(ca, lng_ref[:, hcols], lnb_ref[:, hcols])
                y_s[r:r + ROW_GROUP, hcols] = (
                    _silu(ca) * _silu(za[r:r + ROW_GROUP, lcols])).astype(BF16)
        v = proj(off_gc + c0) * proj(off_hb + c0)
        cbs = []
        for hd, hcols, lcols in head_slabs(c0, MXU_COLS):
            v_s[hd, HALO_B:HALO_B + tm, :] = v[:, lcols]
            cb = None
            for k in range(conv_b_w):
                start = HALO_B - hist_b + k
                term = v_s[hd, start:start + tm, :] * cbw_ref[k:k + 1, hcols]
                cb = term if cb is None else cb + term
            cbs.append(cb)
        y_s[:, d_a + c0:d_a + c0 + MXU_COLS] = (
            proj(off_gb + c0) * jnp.concatenate(cbs, axis=-1) * _silu(proj(off_zb + c0))).astype(BF16)

    for hd, hcols, _ in head_slabs(0, d_a):
        na_ref[:, hcols] = u_s[hd, HALO_A + tm - hist_a:HALO_A + tm, :]
        nb_ref[:, hcols] = v_s[hd, HALO_B + tm - hist_b:HALO_B + tm, :]
    u_s[:, 0:HALO_A, :] = u_s[:, tm:tm + HALO_A, :]
    v_s[:, 0:HALO_B, :] = v_s[:, tm:tm + HALO_B, :]

    out = x + _out_proj(y_s[...], wout_ref)
    if apply_final:
        out = _rms_scale(out) * fg_ref[...]
    y_ref[...] = out


def _const_spec(block_shape, index):
    return pl.BlockSpec(block_shape, lambda *_: index, pipeline_mode=pl.Buffered(1))


def _prompt_layer(x, layer, norm_g, w_in, conv_a_w, conv_a_b, ln_a_g, ln_a_b, conv_b_w, w_out,
                  final_g, *, apply_final):
    bsz, seq, d_model = x.shape
    in_slices = w_in.shape[1]
    conv_a_width, d_a = conv_a_w.shape[1:]
    conv_b_width, d_b = conv_b_w.shape[1:]
    out_slices, d_mix = w_out.shape[1:3]
    tm = PROMPT_TILE
    assert seq % tm == 0 and tm % ROW_GROUP == 0
    assert d_a % MXU_COLS == 0 and d_a == d_b and d_mix == d_a + d_b
    assert conv_a_width - 1 <= HALO_A and conv_b_width - 1 <= HALO_B

    lay = lambda *tail: (layer,) + tail
    in_specs = [
        pl.BlockSpec((None, tm, d_model), lambda b, t: (b, t, 0)),
        _const_spec((None, 1, d_model), lay(0, 0)),
        _const_spec((None, in_slices, d_model, MXU_COLS), lay(0, 0, 0)),
        _const_spec((None, conv_a_width, d_a), lay(0, 0)),
        _const_spec((None, 1, d_a), lay(0, 0)),
        _const_spec((None, 1, d_a), lay(0, 0)),
        _const_spec((None, 1, d_a), lay(0, 0)),
        _const_spec((None, conv_b_width, d_b), lay(0, 0)),
        _const_spec((None, out_slices, d_mix, MXU_COLS), lay(0, 0, 0)),
        _const_spec((1, d_model), (0, 0)),
    ]
    out_specs = [
        pl.BlockSpec((None, tm, d_model), lambda b, t: (b, t, 0)),
        pl.BlockSpec((None, conv_a_width - 1, d_a), lambda b, t: (b, 0, 0)),
        pl.BlockSpec((None, conv_b_width - 1, d_b), lambda b, t: (b, 0, 0)),
    ]
    out_shape = [
        jax.ShapeDtypeStruct((bsz, seq, d_model), F32),
        jax.ShapeDtypeStruct((bsz, conv_a_width - 1, d_a), F32),
        jax.ShapeDtypeStruct((bsz, conv_b_width - 1, d_b), F32),
    ]
    scratch = [
        pltpu.VMEM((tm, d_model), BF16),
        pltpu.VMEM((d_a // HEAD_DIM_A, HALO_A + tm, HEAD_DIM_A), F32),
        pltpu.VMEM((d_b // HEAD_DIM_A, HALO_B + tm, HEAD_DIM_A), F32),
        pltpu.VMEM((tm, d_mix), BF16),
    ]
    return pl.pallas_call(
        functools.partial(_prompt_layer_kernel, apply_final=apply_final),
        grid=(bsz, seq // tm),
        in_specs=in_specs, out_specs=out_specs, out_shape=out_shape, scratch_shapes=scratch,
        compiler_params=pltpu.CompilerParams(
            dimension_semantics=("arbitrary", "arbitrary"), vmem_limit_bytes=VMEM_LIMIT_BYTES),
        name=f"prompt_layer{layer}",
    )(x, norm_g[:, None, :], w_in, conv_a_w, conv_a_b[:, None, :], ln_a_g[:, None, :],
      ln_a_b[:, None, :], conv_b_w, w_out, final_g[None, :])


def _sample_layer_kernel(x_ref, sa_ref, sb_ref, ng_ref, win_ref, caw_ref, cab_ref, lng_ref, lnb_ref,
                         cbw_ref, wout_ref, fg_ref, y_ref, na_ref, nb_ref, y_s, *, apply_final):
    sb, steps, d_model = x_ref.shape
    conv_a_w, d_a = caw_ref.shape
    conv_b_w, d_b = cbw_ref.shape
    off_g, off_za = d_a, 2 * d_a
    off_gb, off_gc, off_hb, off_zb = 3 * d_a, 3 * d_a + d_b, 3 * d_a + 2 * d_b, 3 * d_a + 3 * d_b
    hist_a = conv_a_w - 1
    hist_b = conv_b_w - 1

    x = jnp.concatenate([x_ref[:, t, :] for t in range(steps)], axis=0)
    h = (_rms_scale(x) * ng_ref[...]).astype(BF16)

    def proj(off):
        return jnp.dot(h, win_ref[off // MXU_COLS], preferred_element_type=F32)

    def step_rows(a, t):
        return a[t * sb:(t + 1) * sb]

    for c0 in range(0, d_a, MXU_COLS):
        cols = slice(c0, c0 + MXU_COLS)
        u = proj(c0) * _sigmoid(proj(off_g + c0))
        acc = [None] * steps
        for j in range(hist_a + steps):
            row = sa_ref[:, j, cols] if j < hist_a else step_rows(u, j - hist_a)
            for t in range(steps):
                k = j - t
                if 0 <= k < conv_a_w:
                    term = row * caw_ref[k:k + 1, cols]
                    acc[t] = term if acc[t] is None else acc[t] + term
        ca = jnp.concatenate(acc, axis=0) + cab_ref[:, cols]
        ca = _head_layernorm(ca, lng_ref[:, cols], lnb_ref[:, cols])
        y_s[:, cols] = (_silu(ca) * _silu(proj(off_za + c0))).astype(BF16)
        na_ref[:, 0:hist_a - steps, cols] = sa_ref[:, steps:hist_a, cols]
        for t in range(steps):
            na_ref[:, hist_a - steps + t, cols] = step_rows(u, t)

        v = proj(off_gc + c0) * proj(off_hb + c0)
        accb = [None] * steps
        for j in range(hist_b + steps):
            row = sb_ref[:, j, cols] if j < hist_b else step_rows(v, j - hist_b)
            for t in range(steps):
                k = j - t
                if 0 <= k < conv_b_w:
                    term = row * cbw_ref[k:k + 1, cols]
                    accb[t] = term if accb[t] is None else accb[t] + term
        cb = jnp.concatenate(accb, axis=0)
        y_s[:, d_a + c0:d_a + c0 + MXU_COLS] = (
            proj(off_gb + c0) * cb * _silu(proj(off_zb + c0))).astype(BF16)
        for j in range(hist_b):
            nb_ref[:, j, cols] = step_rows(v, steps - hist_b + j)

    out = x + _out_proj(y_s[...], wout_ref)
    if apply_final:
        out = _rms_scale(out) * fg_ref[...]
    for t in range(steps):
        y_ref[:, t, :] = step_rows(out, t)


def _sample_layer(x, layer, state_a, state_b, norm_g, w_in, conv_a_w, conv_a_b, ln_a_g, ln_a_b,
                  conv_b_w, w_out, final_g, *, apply_final):
    nseq, steps, d_model = x.shape
    in_slices = w_in.shape[1]
    conv_a_width, d_a = conv_a_w.shape[1:]
    conv_b_width, d_b = conv_b_w.shape[1:]
    out_slices, d_mix = w_out.shape[1:3]
    sb = SAMPLE_SEQS
    assert nseq % sb == 0 and sb % SUBLANES == 0
    assert steps <= conv_b_width - 1 + steps and steps <= conv_a_width - 1
    assert d_a % MXU_COLS == 0 and d_a == d_b and d_mix == d_a + d_b

    lay = lambda *tail: (layer,) + tail
    in_specs = [
        pl.BlockSpec((sb, steps, d_model), lambda i: (i, 0, 0)),
        pl.BlockSpec((None, sb, conv_a_width - 1, d_a), lambda i: (layer, i, 0, 0)),
        pl.BlockSpec((None, sb, conv_b_width - 1, d_b), lambda i: (layer, i, 0, 0)),
        _const_spec((None, 1, d_model), lay(0, 0)),
        _const_spec((None, in_slices, d_model, MXU_COLS), lay(0, 0, 0)),
        _const_spec((None, conv_a_width, d_a), lay(0, 0)),
        _const_spec((None, 1, d_a), lay(0, 0)),
        _const_spec((None, 1, d_a), lay(0, 0)),
        _const_spec((None, 1, d_a), lay(0, 0)),
        _const_spec((None, conv_b_width, d_b), lay(0, 0)),
        _const_spec((None, out_slices, d_mix, MXU_COLS), lay(0, 0, 0)),
        _const_spec((1, d_model), (0, 0)),
    ]
    out_specs = [
        pl.BlockSpec((sb, steps, d_model), lambda i: (i, 0, 0)),
        pl.BlockSpec((sb, conv_a_width - 1, d_a), lambda i: (i, 0, 0)),
        pl.BlockSpec((sb, conv_b_width - 1, d_b), lambda i: (i, 0, 0)),
    ]
    out_shape = [
        jax.ShapeDtypeStruct((nseq, steps, d_model), F32),
        jax.ShapeDtypeStruct((nseq, conv_a_width - 1, d_a), F32),
        jax.ShapeDtypeStruct((nseq, conv_b_width - 1, d_b), F32),
    ]
    return pl.pallas_call(
        functools.partial(_sample_layer_kernel, apply_final=apply_final),
        grid=(nseq // sb,),
        in_specs=in_specs, out_specs=out_specs, out_shape=out_shape,
        scratch_shapes=[pltpu.VMEM((sb * steps, d_mix), BF16)],
        compiler_params=pltpu.CompilerParams(
            dimension_semantics=("arbitrary",), vmem_limit_bytes=VMEM_LIMIT_BYTES),
        name=f"sample_layer{layer}",
    )(x, state_a, state_b, norm_g[:, None, :], w_in, conv_a_w, conv_a_b[:, None, :],
      ln_a_g[:, None, :], ln_a_b[:, None, :], conv_b_w, w_out, final_g[None, :])


def _column_slices(w):
    depth, rows, cols = w.shape
    return w.reshape(depth, rows, cols // MXU_COLS, MXU_COLS).transpose(0, 2, 1, 3)


def kernel(x_prompt, x_sample, state_conv_a, state_conv_b, norm_g, w_in, conv_a_w, conv_a_b,
           ln_a_g, ln_a_b, conv_b_w, w_out, final_g):
    depth = w_in.shape[0]
    w_in_bf = _column_slices(w_in.astype(BF16))
    w_out_bf = _column_slices(w_out.astype(BF16))
    weights = (norm_g, w_in_bf, conv_a_w, conv_a_b, ln_a_g, ln_a_b, conv_b_w, w_out_bf, final_g)

    xp, xs = x_prompt, x_sample
    pa, pb, sa, sb = [], [], [], []
    for layer in range(depth):
        last = layer == depth - 1
        xp, na, nb = _prompt_layer(xp, layer, *weights, apply_final=last)
        pa.append(na)
        pb.append(nb)
        xs, na, nb = _sample_layer(xs, layer, state_conv_a, state_conv_b, *weights, apply_final=last)
        sa.append(na)
        sb.append(nb)
    return (xp, xs, jnp.stack(pa), jnp.stack(pb), jnp.stack(sa), jnp.stack(sb))
```

```python
import functools

import jax
import jax.numpy as jnp
from jax import lax
from jax.experimental import pallas as pl
from jax.experimental.pallas import tpu as pltpu

F32 = jnp.float32
BF16 = jnp.bfloat16

HEAD_DIM_A = 128
RMS_EPS = 1e-6
LN_EPS = 1e-5

SUBLANES = 8
MXU_COLS = 256
HALO_A = 32
HALO_B = 8
ROW_GROUP = 16
PROMPT_TILE = 512
SAMPLE_SEQS = 32
VMEM_LIMIT_BYTES = 56 * 1024 * 1024
WEIGHT_PREP_COLS = 1024


def _sigmoid(x):
    return 1.0 / (1.0 + jnp.exp(-x))


def _silu(x):
    return x * _sigmoid(x)


def _rms_scale(x):
    return x * lax.rsqrt(jnp.mean(x * x, axis=-1, keepdims=True) + RMS_EPS)


def _head_layernorm(ca, g, b):
    outs = []
    for j in range(ca.shape[-1] // HEAD_DIM_A):
        c = ca[:, j * HEAD_DIM_A:(j + 1) * HEAD_DIM_A]
        mu = jnp.mean(c, axis=-1, keepdims=True)
        d = c - mu
        var = jnp.mean(d * d, axis=-1, keepdims=True)
        outs.append(d * lax.rsqrt(var + LN_EPS))
    return jnp.concatenate(outs, axis=-1) * g + b


def _mix_col(c0):
    return 2 * c0


def _out_proj(y, wout_ref):
    return jnp.concatenate(
        [jnp.dot(y, wout_ref[n], preferred_element_type=F32) for n in range(wout_ref.shape[0])], axis=-1)


def _prompt_layer_kernel(x_ref, ng_ref, win_ref, caw_ref, cab_ref, lng_ref, lnb_ref, cbw_ref,
                         wout_ref, fg_ref, y_ref, na_ref, nb_ref,
                         h_s, u_s, v_s, p_s, y_s, *, apply_final):
    tm, d_model = x_ref.shape
    conv_a_w, d_a = caw_ref.shape
    conv_b_w, d_b = cbw_ref.shape
    off_g, off_za = d_a, 2 * d_a
    off_gb, off_gc, off_hb, off_zb = 3 * d_a, 3 * d_a + d_b, 3 * d_a + 2 * d_b, 3 * d_a + 3 * d_b
    hist_a = conv_a_w - 1
    hist_b = conv_b_w - 1

    n_heads = d_a // HEAD_DIM_A

    def head_slabs(c0, width):
        return [(hc // HEAD_DIM_A, slice(hc, hc + HEAD_DIM_A), slice(hc - c0, hc - c0 + HEAD_DIM_A))
                for hc in range(c0, c0 + width, HEAD_DIM_A)]

    @pl.when(pl.program_id(1) == 0)
    def _():
        u_s[:, 0:HALO_A, :] = jnp.zeros((n_heads, HALO_A, HEAD_DIM_A), F32)
        v_s[:, 0:HALO_B, :] = jnp.zeros((n_heads, HALO_B, HEAD_DIM_A), F32)

    x = x_ref[...]
    h_s[...] = (_rms_scale(x) * ng_ref[...]).astype(BF16)

    def proj(off):
        return jnp.dot(h_s[...], win_ref[off // MXU_COLS], preferred_element_type=F32)

    def project(c0, slot):
        u = proj(c0) * _sigmoid(proj(off_g + c0))
        v = proj(off_gc + c0) * proj(off_hb + c0)
        for hd, _, lcols in head_slabs(c0, MXU_COLS):
            u_s[hd, HALO_A:HALO_A + tm, :] = u[:, lcols]
            v_s[hd, HALO_B:HALO_B + tm, :] = v[:, lcols]
        p_s[slot, 0] = proj(off_za + c0)
        p_s[slot, 1] = proj(off_gb + c0)
        p_s[slot, 2] = proj(off_zb + c0)

    def mix(c0, slot):
        for hd, hcols, lcols in head_slabs(c0, MXU_COLS):
            taps = [jnp.broadcast_to(caw_ref[k:k + 1, hcols], (SUBLANES, HEAD_DIM_A))
                    for k in range(conv_a_w)]
            ycol = _mix_col(c0) + lcols.start
            for r in range(0, tm, ROW_GROUP):
                accs = []
                for s in range(r, r + ROW_GROUP, SUBLANES):
                    acc = None
                    for k in range(conv_a_w):
                        start = s + HALO_A - hist_a + k
                        term = u_s[hd, start:start + SUBLANES, :] * taps[k]
                        acc = term if acc is None else acc + term
                    accs.append(acc)
                ca = jnp.concatenate(accs, axis=0) + cab_ref[:, hcols]
                ca = _head_layernorm(ca, lng_ref[:, hcols], lnb_ref[:, hcols])
                y_s[r:r + ROW_GROUP, ycol:ycol + HEAD_DIM_A] = (
                    _silu(ca) * _silu(p_s[slot, 0, r:r + ROW_GROUP, lcols])).astype(BF16)
        cbs = []
        for hd, hcols, _ in head_slabs(c0, MXU_COLS):
            cb = None
            for k in range(conv_b_w):
                start = HALO_B - hist_b + k
                term = v_s[hd, start:start + tm, :] * cbw_ref[k:k + 1, hcols]
                cb = term if cb is None else cb + term
            cbs.append(cb)
        ycol = _mix_col(c0) + MXU_COLS
        y_s[:, ycol:ycol + MXU_COLS] = (
            p_s[slot, 1] * jnp.concatenate(cbs, axis=-1) * _silu(p_s[slot, 2])).astype(BF16)

    def project_out(c0):
        rows = slice(_mix_col(c0), _mix_col(c0) + 2 * MXU_COLS)
        for n in range(wout_ref.shape[0]):
            ncols = slice(n * MXU_COLS, (n + 1) * MXU_COLS)
            part = jnp.dot(y_s[:, rows], wout_ref[n, rows, :], preferred_element_type=F32)
            y_ref[:, ncols] = (x_ref[:, ncols] if c0 == 0 else y_ref[:, ncols]) + part

    blocks = list(range(0, d_a, MXU_COLS))
    project(blocks[0], 0)
    for i, c0 in enumerate(blocks):
        if i + 1 < len(blocks):
            project(blocks[i + 1], (i + 1) % 2)
        mix(c0, i % 2)
        project_out(c0)

    for hd, hcols, _ in head_slabs(0, d_a):
        na_ref[:, hcols] = u_s[hd, HALO_A + tm - hist_a:HALO_A + tm, :]
        nb_ref[:, hcols] = v_s[hd, HALO_B + tm - hist_b:HALO_B + tm, :]
    u_s[:, 0:HALO_A, :] = u_s[:, tm:tm + HALO_A, :]
    v_s[:, 0:HALO_B, :] = v_s[:, tm:tm + HALO_B, :]

    if apply_final:
        y_ref[...] = _rms_scale(y_ref[...]) * fg_ref[...]


def _const_spec(block_shape, index):
    return pl.BlockSpec(block_shape, lambda *_: index, pipeline_mode=pl.Buffered(1))


def _prompt_layer(x, layer, norm_g, w_in, conv_a_w, conv_a_b, ln_a_g, ln_a_b, conv_b_w, w_out,
                  final_g, *, apply_final):
    bsz, seq, d_model = x.shape
    in_slices = w_in.shape[1]
    conv_a_width, d_a = conv_a_w.shape[1:]
    conv_b_width, d_b = conv_b_w.shape[1:]
    out_slices, d_mix = w_out.shape[1:3]
    tm = PROMPT_TILE
    assert seq % tm == 0 and tm % ROW_GROUP == 0
    assert d_a % MXU_COLS == 0 and d_a == d_b and d_mix == d_a + d_b
    assert conv_a_width - 1 <= HALO_A and conv_b_width - 1 <= HALO_B

    lay = lambda *tail: (layer,) + tail
    in_specs = [
        pl.BlockSpec((None, tm, d_model), lambda b, t: (b, t, 0)),
        _const_spec((None, 1, d_model), lay(0, 0)),
        _const_spec((None, in_slices, d_model, MXU_COLS), lay(0, 0, 0)),
        _const_spec((None, conv_a_width, d_a), lay(0, 0)),
        _const_spec((None, 1, d_a), lay(0, 0)),
        _const_spec((None, 1, d_a), lay(0, 0)),
        _const_spec((None, 1, d_a), lay(0, 0)),
        _const_spec((None, conv_b_width, d_b), lay(0, 0)),
        _const_spec((None, out_slices, d_mix, MXU_COLS), lay(0, 0, 0)),
        _const_spec((1, d_model), (0, 0)),
    ]
    out_specs = [
        pl.BlockSpec((None, tm, d_model), lambda b, t: (b, t, 0)),
        pl.BlockSpec((None, conv_a_width - 1, d_a), lambda b, t: (b, 0, 0)),
        pl.BlockSpec((None, conv_b_width - 1, d_b), lambda b, t: (b, 0, 0)),
    ]
    out_shape = [
        jax.ShapeDtypeStruct((bsz, seq, d_model), F32),
        jax.ShapeDtypeStruct((bsz, conv_a_width - 1, d_a), F32),
        jax.ShapeDtypeStruct((bsz, conv_b_width - 1, d_b), F32),
    ]
    scratch = [
        pltpu.VMEM((tm, d_model), BF16),
        pltpu.VMEM((d_a // HEAD_DIM_A, HALO_A + tm, HEAD_DIM_A), F32),
        pltpu.VMEM((d_b // HEAD_DIM_A, HALO_B + tm, HEAD_DIM_A), F32),
        pltpu.VMEM((2, 3, tm, MXU_COLS), F32),
        pltpu.VMEM((tm, d_mix), BF16),
    ]
    return pl.pallas_call(
        functools.partial(_prompt_layer_kernel, apply_final=apply_final),
        grid=(bsz, seq // tm),
        in_specs=in_specs, out_specs=out_specs, out_shape=out_shape, scratch_shapes=scratch,
        compiler_params=pltpu.CompilerParams(
            dimension_semantics=("arbitrary", "arbitrary"), vmem_limit_bytes=VMEM_LIMIT_BYTES),
        name=f"prompt_layer{layer}",
    )(x, norm_g[:, None, :], w_in, conv_a_w, conv_a_b[:, None, :], ln_a_g[:, None, :],
      ln_a_b[:, None, :], conv_b_w, w_out, final_g[None, :])


def _sample_kernel(*refs, steps, n_heads):
    x_ref = refs[0]
    sa_refs = refs[1:1 + n_heads]
    sb_refs = refs[1 + n_heads:1 + 2 * n_heads]
    (ng_ref, win_ref, caw_ref, cab_ref, lng_ref, lnb_ref, cbw_ref, wout_ref, fg_ref,
     y_ref, na_ref, nb_ref, xs_s, u_s, v_s, ca_s, cb_s, y_s) = refs[1 + 2 * n_heads:]
    layer, blk = pl.program_id(0), pl.program_id(1)
    rows, d_model = x_ref.shape
    nseq = rows // steps
    conv_a_w, d_a = caw_ref.shape
    conv_b_w, d_b = cbw_ref.shape
    off_g, off_za = d_a, 2 * d_a
    off_gb, off_gc, off_hb, off_zb = 3 * d_a, 3 * d_a + d_b, 3 * d_a + 2 * d_b, 3 * d_a + 3 * d_b
    hist_a = conv_a_w - 1
    hist_b = conv_b_w - 1

    @pl.when(layer == 0)
    def _():
        xs_s[blk] = x_ref[...]

    x = xs_s[blk]
    h = (_rms_scale(x) * ng_ref[...]).astype(BF16)

    def proj(off):
        return jnp.dot(h, win_ref[off // MXU_COLS], preferred_element_type=F32)

    def causal_taps(hist_ref, new_s, hd, hist, w_ref, hcols, out_s):
        new = [new_s[hd, pl.ds(t, nseq, stride=steps), :] for t in range(steps)]
        acc = [None] * steps
        for j in range(hist + steps):
            row = hist_ref[pl.ds(j, nseq, stride=hist), :] if j < hist else new[j - hist]
            for t in range(steps):
                k = j - t
                if 0 <= k <= hist:
                    term = row * w_ref[k:k + 1, hcols]
                    acc[t] = term if acc[t] is None else acc[t] + term
        for t in range(steps):
            out_s[hd, pl.ds(t, nseq, stride=steps), :] = acc[t]

    for c0 in range(0, d_a, MXU_COLS):
        cols = slice(c0, c0 + MXU_COLS)
        heads = [(hc // HEAD_DIM_A, slice(hc, hc + HEAD_DIM_A), slice(hc - c0, hc - c0 + HEAD_DIM_A))
                 for hc in range(c0, c0 + MXU_COLS, HEAD_DIM_A)]
        u = proj(c0) * _sigmoid(proj(off_g + c0))
        v = proj(off_gc + c0) * proj(off_hb + c0)
        for hd, hcols, lcols in heads:
            u_s[hd] = u[:, lcols]
            v_s[hd] = v[:, lcols]
            causal_taps(sa_refs[hd], u_s, hd, hist_a, caw_ref, hcols, ca_s)
            causal_taps(sb_refs[hd], v_s, hd, hist_b, cbw_ref, hcols, cb_s)
            na_ref[0:nseq * hist_a - steps, hcols] = sa_refs[hd][steps:nseq * hist_a, :]
            for q in range(nseq):
                na_ref[(q + 1) * hist_a - steps:(q + 1) * hist_a, hcols] = u_s[hd, q * steps:(q + 1) * steps, :]
                nb_ref[q * hist_b:(q + 1) * hist_b, hcols] = v_s[hd, (q + 1) * steps - hist_b:(q + 1) * steps, :]
        ca = jnp.concatenate([ca_s[hd] for hd, _, _ in heads], axis=-1) + cab_ref[:, cols]
        ca = _head_layernorm(ca, lng_ref[:, cols], lnb_ref[:, cols])
        y_s[:, _mix_col(c0):_mix_col(c0) + MXU_COLS] = (_silu(ca) * _silu(proj(off_za + c0))).astype(BF16)
        cb = jnp.concatenate([cb_s[hd] for hd, _, _ in heads], axis=-1)
        y_s[:, _mix_col(c0) + MXU_COLS:_mix_col(c0) + 2 * MXU_COLS] = (
            proj(off_gb + c0) * cb * _silu(proj(off_zb + c0))).astype(BF16)

    out = x + _out_proj(y_s[...], wout_ref)
    xs_s[blk] = out
    is_last = layer == pl.num_programs(0) - 1
    y_ref[...] = jnp.where(is_last, _rms_scale(out) * fg_ref[...], out)


def _sample_trunk(x, state_a, state_b, norm_g, w_in, conv_a_w, conv_a_b, ln_a_g, ln_a_b, conv_b_w,
                  w_out, final_g):
    nseq, steps, d_model = x.shape
    depth, in_slices = w_in.shape[:2]
    conv_a_width, d_a = conv_a_w.shape[1:]
    conv_b_width, d_b = conv_b_w.shape[1:]
    out_slices, d_mix = w_out.shape[1:3]
    hist_a, hist_b = conv_a_width - 1, conv_b_width - 1
    n_heads = d_a // HEAD_DIM_A
    sb = SAMPLE_SEQS
    nblk = nseq // sb
    assert nseq % sb == 0 and (sb * hist_b) % SUBLANES == 0
    assert hist_b <= steps <= hist_a
    assert d_a % MXU_COLS == 0 and d_a == d_b and d_mix == d_a + d_b

    def per_layer(*block):
        return pl.BlockSpec((None,) + block, lambda l, i: (l,) + (0,) * len(block),
                            pipeline_mode=pl.Buffered(1))

    def head_slab(hist, hd):
        return pl.BlockSpec((None, sb * hist, HEAD_DIM_A), lambda l, i: (l, i, hd))

    in_specs = (
        [pl.BlockSpec((sb * steps, d_model), lambda l, i: (i, 0))]
        + [head_slab(hist_a, hd) for hd in range(n_heads)]
        + [head_slab(hist_b, hd) for hd in range(n_heads)]
        + [per_layer(1, d_model),
           per_layer(in_slices, d_model, MXU_COLS),
           per_layer(conv_a_width, d_a), per_layer(1, d_a), per_layer(1, d_a), per_layer(1, d_a),
           per_layer(conv_b_width, d_b),
           per_layer(out_slices, d_mix, MXU_COLS),
           pl.BlockSpec((1, d_model), lambda l, i: (0, 0))])
    out_specs = [
        pl.BlockSpec((None, sb * steps, d_model), lambda l, i: (l, i, 0)),
        pl.BlockSpec((None, sb * hist_a, d_a), lambda l, i: (l, i, 0)),
        pl.BlockSpec((None, sb * hist_b, d_b), lambda l, i: (l, i, 0)),
    ]
    out_shape = [
        jax.ShapeDtypeStruct((depth, nseq * steps, d_model), F32),
        jax.ShapeDtypeStruct((depth, nseq * hist_a, d_a), F32),
        jax.ShapeDtypeStruct((depth, nseq * hist_b, d_b), F32),
    ]
    slab = lambda: pltpu.VMEM((n_heads, sb * steps, HEAD_DIM_A), F32)
    scratch = [
        pltpu.VMEM((nblk, sb * steps, d_model), F32),
        slab(), slab(), slab(), slab(),
        pltpu.VMEM((sb * steps, d_mix), BF16),
    ]
    sa2 = state_a.reshape(depth, nseq * hist_a, d_a)
    sb2 = state_b.reshape(depth, nseq * hist_b, d_b)
    y, na, nb = pl.pallas_call(
        functools.partial(_sample_kernel, steps=steps, n_heads=n_heads),
        grid=(depth, nblk),
        in_specs=in_specs, out_specs=out_specs, out_shape=out_shape, scratch_shapes=scratch,
        compiler_params=pltpu.CompilerParams(
            dimension_semantics=("arbitrary", "arbitrary"), vmem_limit_bytes=VMEM_LIMIT_BYTES),
        name="sample_trunk",
    )(x.reshape(nseq * steps, d_model), *([sa2] * n_heads), *([sb2] * n_heads),
      norm_g[:, None, :], w_in, conv_a_w, conv_a_b[:, None, :], ln_a_g[:, None, :], ln_a_b[:, None, :],
      conv_b_w, w_out, final_g[None, :])
    return (y[depth - 1].reshape(nseq, steps, d_model), na.reshape(depth, nseq, hist_a, d_a),
            nb.reshape(depth, nseq, hist_b, d_b))


def _slice_cast_kernel(w_ref, o_ref, *, row_blocks):
    for j in range(o_ref.shape[0]):
        cols = slice(j * MXU_COLS, (j + 1) * MXU_COLS)
        r = 0
        for start, size in row_blocks:
            o_ref[j, r:r + size, :] = w_ref[start:start + size, cols].astype(BF16)
            r += size


def _column_slices(w, row_blocks=None):
    depth, rows, cols = w.shape
    row_blocks = row_blocks or ((0, rows),)
    group = WEIGHT_PREP_COLS // MXU_COLS
    return pl.pallas_call(
        functools.partial(_slice_cast_kernel, row_blocks=row_blocks),
        grid=(depth, cols // WEIGHT_PREP_COLS),
        in_specs=[pl.BlockSpec((None, rows, WEIGHT_PREP_COLS), lambda l, j: (l, 0, j))],
        out_specs=pl.BlockSpec((None, group, rows, MXU_COLS), lambda l, j: (l, j, 0, 0)),
        out_shape=jax.ShapeDtypeStruct((depth, cols // MXU_COLS, rows, MXU_COLS), BF16),
        compiler_params=pltpu.CompilerParams(
            dimension_semantics=("arbitrary", "arbitrary"), vmem_limit_bytes=VMEM_LIMIT_BYTES),
        name="weight_slices",
    )(w)


def _mix_row_blocks(d_a, d_mix):
    return tuple((g * d_a + c0, MXU_COLS) for c0 in range(0, d_a, MXU_COLS) for g in range(d_mix // d_a))


def kernel(x_prompt, x_sample, state_conv_a, state_conv_b, norm_g, w_in, conv_a_w, conv_a_b,
           ln_a_g, ln_a_b, conv_b_w, w_out, final_g):
    depth = w_in.shape[0]
    w_in_bf = _column_slices(w_in)
    w_out_bf = _column_slices(w_out, _mix_row_blocks(conv_a_w.shape[2], w_out.shape[1]))
    weights = (norm_g, w_in_bf, conv_a_w, conv_a_b, ln_a_g, ln_a_b, conv_b_w, w_out_bf, final_g)

    xp = x_prompt
    pa, pb = [], []
    for layer in range(depth):
        xp, na, nb = _prompt_layer(xp, layer, *weights, apply_final=layer == depth - 1)
        pa.append(na)
        pb.append(nb)
    ys, sa, sb = _sample_trunk(x_sample, state_conv_a, state_conv_b, *weights)
    return (xp, ys, jnp.stack(pa), jnp.stack(pb), sa, sb)
```

```python
import functools

import jax
import jax.numpy as jnp
from jax import lax
from jax.experimental import pallas as pl
from jax.experimental.pallas import tpu as pltpu

F32 = jnp.float32
BF16 = jnp.bfloat16

HEAD_DIM_A = 128
RMS_EPS = 1e-6
LN_EPS = 1e-5

SUBLANES = 8
MXU_COLS = 256
HALO_A = 32
HALO_B = 8
ROW_GROUP = 16
PROMPT_TILE = 512
SAMPLE_SEQS = 32
SAMPLE_ROWS_A = 40
VMEM_LIMIT_BYTES = 56 * 1024 * 1024
WEIGHT_PREP_COLS = 1024


def _sigmoid(x):
    return 1.0 / (1.0 + jnp.exp(-x))


def _silu(x):
    return x * _sigmoid(x)


def _rms_scale(x):
    return x * lax.rsqrt(jnp.mean(x * x, axis=-1, keepdims=True) + RMS_EPS)


def _head_layernorm(ca, g, b):
    outs = []
    for j in range(ca.shape[-1] // HEAD_DIM_A):
        c = ca[:, j * HEAD_DIM_A:(j + 1) * HEAD_DIM_A]
        mu = jnp.mean(c, axis=-1, keepdims=True)
        d = c - mu
        var = jnp.mean(d * d, axis=-1, keepdims=True)
        outs.append(d * lax.rsqrt(var + LN_EPS))
    return jnp.concatenate(outs, axis=-1) * g + b


def _mix_col(c0):
    return 2 * c0


def _out_proj(y, wout_ref):
    return jnp.concatenate(
        [jnp.dot(y, wout_ref[n], preferred_element_type=F32) for n in range(wout_ref.shape[0])], axis=-1)


def _prompt_layer_kernel(x_ref, ng_ref, win_ref, caw_ref, cab_ref, lng_ref, lnb_ref, cbw_ref,
                         wout_ref, fg_ref, y_ref, na_ref, nb_ref,
                         h_s, u_s, v_s, p_s, y_s, *, apply_final):
    tm, d_model = x_ref.shape
    conv_a_w, d_a = caw_ref.shape
    conv_b_w, d_b = cbw_ref.shape
    off_g, off_za = d_a, 2 * d_a
    off_gb, off_gc, off_hb, off_zb = 3 * d_a, 3 * d_a + d_b, 3 * d_a + 2 * d_b, 3 * d_a + 3 * d_b
    hist_a = conv_a_w - 1
    hist_b = conv_b_w - 1

    n_heads = d_a // HEAD_DIM_A

    def head_slabs(c0, width):
        return [(hc // HEAD_DIM_A, slice(hc, hc + HEAD_DIM_A), slice(hc - c0, hc - c0 + HEAD_DIM_A))
                for hc in range(c0, c0 + width, HEAD_DIM_A)]

    @pl.when(pl.program_id(1) == 0)
    def _():
        u_s[:, 0:HALO_A, :] = jnp.zeros((n_heads, HALO_A, HEAD_DIM_A), F32)
        v_s[:, 0:HALO_B, :] = jnp.zeros((n_heads, HALO_B, HEAD_DIM_A), F32)

    x = x_ref[...]
    h_s[...] = (_rms_scale(x) * ng_ref[...]).astype(BF16)

    def proj(off):
        return jnp.dot(h_s[...], win_ref[off // MXU_COLS], preferred_element_type=F32)

    def project(c0, slot):
        u = proj(c0) * _sigmoid(proj(off_g + c0))
        v = proj(off_gc + c0) * proj(off_hb + c0)
        for hd, _, lcols in head_slabs(c0, MXU_COLS):
            u_s[hd, HALO_A:HALO_A + tm, :] = u[:, lcols]
            v_s[hd, HALO_B:HALO_B + tm, :] = v[:, lcols]
        p_s[slot, 0] = proj(off_za + c0)
        p_s[slot, 1] = proj(off_gb + c0)
        p_s[slot, 2] = proj(off_zb + c0)

    def mix(c0, slot):
        for hd, hcols, lcols in head_slabs(c0, MXU_COLS):
            taps = [jnp.broadcast_to(caw_ref[k:k + 1, hcols], (SUBLANES, HEAD_DIM_A))
                    for k in range(conv_a_w)]
            ycol = _mix_col(c0) + lcols.start
            for r in range(0, tm, ROW_GROUP):
                accs = []
                for s in range(r, r + ROW_GROUP, SUBLANES):
                    acc = None
                    for k in range(conv_a_w):
                        start = s + HALO_A - hist_a + k
                        term = u_s[hd, start:start + SUBLANES, :] * taps[k]
                        acc = term if acc is None else acc + term
                    accs.append(acc)
                ca = jnp.concatenate(accs, axis=0) + cab_ref[:, hcols]
                ca = _head_layernorm(ca, lng_ref[:, hcols], lnb_ref[:, hcols])
                y_s[r:r + ROW_GROUP, ycol:ycol + HEAD_DIM_A] = (
                    _silu(ca) * _silu(p_s[slot, 0, r:r + ROW_GROUP, lcols])).astype(BF16)
        cbs = []
        for hd, hcols, _ in head_slabs(c0, MXU_COLS):
            cb = None
            for k in range(conv_b_w):
                start = HALO_B - hist_b + k
                term = v_s[hd, start:start + tm, :] * cbw_ref[k:k + 1, hcols]
                cb = term if cb is None else cb + term
            cbs.append(cb)
        ycol = _mix_col(c0) + MXU_COLS
        y_s[:, ycol:ycol + MXU_COLS] = (
            p_s[slot, 1] * jnp.concatenate(cbs, axis=-1) * _silu(p_s[slot, 2])).astype(BF16)

    def project_out(c0):
        rows = slice(_mix_col(c0), _mix_col(c0) + 2 * MXU_COLS)
        for n in range(wout_ref.shape[0]):
            ncols = slice(n * MXU_COLS, (n + 1) * MXU_COLS)
            part = jnp.dot(y_s[:, rows], wout_ref[n, rows, :], preferred_element_type=F32)
            y_ref[:, ncols] = (x_ref[:, ncols] if c0 == 0 else y_ref[:, ncols]) + part

    blocks = list(range(0, d_a, MXU_COLS))
    project(blocks[0], 0)
    for i, c0 in enumerate(blocks):
        if i + 1 < len(blocks):
            project(blocks[i + 1], (i + 1) % 2)
        mix(c0, i % 2)
        project_out(c0)

    for hd, hcols, _ in head_slabs(0, d_a):
        na_ref[:, hcols] = u_s[hd, HALO_A + tm - hist_a:HALO_A + tm, :]
        nb_ref[:, hcols] = v_s[hd, HALO_B + tm - hist_b:HALO_B + tm, :]
    u_s[:, 0:HALO_A, :] = u_s[:, tm:tm + HALO_A, :]
    v_s[:, 0:HALO_B, :] = v_s[:, tm:tm + HALO_B, :]

    if apply_final:
        y_ref[...] = _rms_scale(y_ref[...]) * fg_ref[...]


def _const_spec(block_shape, index):
    return pl.BlockSpec(block_shape, lambda *_: index, pipeline_mode=pl.Buffered(1))


def _prompt_layer(x, layer, norm_g, w_in, conv_a_w, conv_a_b, ln_a_g, ln_a_b, conv_b_w, w_out,
                  final_g, *, apply_final):
    bsz, seq, d_model = x.shape
    in_slices = w_in.shape[1]
    conv_a_width, d_a = conv_a_w.shape[1:]
    conv_b_width, d_b = conv_b_w.shape[1:]
    out_slices, d_mix = w_out.shape[1:3]
    tm = PROMPT_TILE
    assert seq % tm == 0 and tm % ROW_GROUP == 0
    assert d_a % MXU_COLS == 0 and d_a == d_b and d_mix == d_a + d_b
    assert conv_a_width - 1 <= HALO_A and conv_b_width - 1 <= HALO_B

    lay = lambda *tail: (layer,) + tail
    in_specs = [
        pl.BlockSpec((None, tm, d_model), lambda b, t: (b, t, 0)),
        _const_spec((None, 1, d_model), lay(0, 0)),
        _const_spec((None, in_slices, d_model, MXU_COLS), lay(0, 0, 0)),
        _const_spec((None, conv_a_width, d_a), lay(0, 0)),
        _const_spec((None, 1, d_a), lay(0, 0)),
        _const_spec((None, 1, d_a), lay(0, 0)),
        _const_spec((None, 1, d_a), lay(0, 0)),
        _const_spec((None, conv_b_width, d_b), lay(0, 0)),
        _const_spec((None, out_slices, d_mix, MXU_COLS), lay(0, 0, 0)),
        _const_spec((1, d_model), (0, 0)),
    ]
    out_specs = [
        pl.BlockSpec((None, tm, d_model), lambda b, t: (b, t, 0)),
        pl.BlockSpec((None, conv_a_width - 1, d_a), lambda b, t: (b, 0, 0)),
        pl.BlockSpec((None, conv_b_width - 1, d_b), lambda b, t: (b, 0, 0)),
    ]
    out_shape = [
        jax.ShapeDtypeStruct((bsz, seq, d_model), F32),
        jax.ShapeDtypeStruct((bsz, conv_a_width - 1, d_a), F32),
        jax.ShapeDtypeStruct((bsz, conv_b_width - 1, d_b), F32),
    ]
    scratch = [
        pltpu.VMEM((tm, d_model), BF16),
        pltpu.VMEM((d_a // HEAD_DIM_A, HALO_A + tm, HEAD_DIM_A), F32),
        pltpu.VMEM((d_b // HEAD_DIM_A, HALO_B + tm, HEAD_DIM_A), F32),
        pltpu.VMEM((2, 3, tm, MXU_COLS), F32),
        pltpu.VMEM((tm, d_mix), BF16),
    ]
    return pl.pallas_call(
        functools.partial(_prompt_layer_kernel, apply_final=apply_final),
        grid=(bsz, seq // tm),
        in_specs=in_specs, out_specs=out_specs, out_shape=out_shape, scratch_shapes=scratch,
        compiler_params=pltpu.CompilerParams(
            dimension_semantics=("arbitrary", "arbitrary"), vmem_limit_bytes=VMEM_LIMIT_BYTES),
        name=f"prompt_layer{layer}",
    )(x, norm_g[:, None, :], w_in, conv_a_w, conv_a_b[:, None, :], ln_a_g[:, None, :],
      ln_a_b[:, None, :], conv_b_w, w_out, final_g[None, :])


def _sample_kernel(*refs, steps, n_heads):
    x_ref = refs[0]
    sa_refs = refs[1:1 + n_heads]
    sb_refs = refs[1 + n_heads:1 + 2 * n_heads]
    (ng_ref, win_ref, caw_ref, cab_ref, lng_ref, lnb_ref, cbw_ref, wout_ref, fg_ref,
     y_ref, na_ref, nb_ref, xs_s, xa_s, u_s, v_s, ca_s, cb_s, y_s) = refs[1 + 2 * n_heads:]
    layer, blk = pl.program_id(0), pl.program_id(1)
    rows, d_model = x_ref.shape
    nseq = rows // steps
    conv_a_w, d_a = caw_ref.shape
    conv_b_w, d_b = cbw_ref.shape
    off_g, off_za = d_a, 2 * d_a
    off_gb, off_gc, off_hb, off_zb = 3 * d_a, 3 * d_a + d_b, 3 * d_a + 2 * d_b, 3 * d_a + 3 * d_b
    hist_a = conv_a_w - 1
    hist_b = conv_b_w - 1

    @pl.when(layer == 0)
    def _():
        xs_s[blk] = x_ref[...]

    x = xs_s[blk]
    h = (_rms_scale(x) * ng_ref[...]).astype(BF16)

    def proj(off):
        return jnp.dot(h, win_ref[off // MXU_COLS], preferred_element_type=F32)

    def conv_a(hd, hcols):
        xa_s[hd, :, 0:hist_a, :] = sa_refs[hd][:, 0:hist_a, :]
        xa_s[hd, :, SAMPLE_ROWS_A - SUBLANES:SAMPLE_ROWS_A, :] = jnp.zeros((nseq, SUBLANES, HEAD_DIM_A), F32)
        taps = [jnp.broadcast_to(caw_ref[k:k + 1, hcols], (SUBLANES, HEAD_DIM_A)) for k in range(conv_a_w)]
        for q in range(nseq):
            xa_s[hd, q, hist_a:hist_a + steps, :] = u_s[hd, q * steps:(q + 1) * steps, :]
        for q in range(nseq):
            acc = None
            for k in range(conv_a_w):
                term = xa_s[hd, q, k:k + SUBLANES, :] * taps[k]
                acc = term if acc is None else acc + term
            ca_s[hd, q * steps:(q + 1) * steps, :] = acc[0:steps]
        na_ref[:, 0:hist_a, hcols] = xa_s[hd, :, steps:steps + hist_a, :]

    def causal_taps(hist_ref, new_s, hd, hist, w_ref, hcols, out_s):
        new = [new_s[hd, pl.ds(t, nseq, stride=steps), :] for t in range(steps)]
        acc = [None] * steps
        for j in range(hist + steps):
            row = hist_ref[pl.ds(j, nseq, stride=hist), :] if j < hist else new[j - hist]
            for t in range(steps):
                k = j - t
                if 0 <= k <= hist:
                    term = row * w_ref[k:k + 1, hcols]
                    acc[t] = term if acc[t] is None else acc[t] + term
        for t in range(steps):
            out_s[hd, pl.ds(t, nseq, stride=steps), :] = acc[t]

    for c0 in range(0, d_a, MXU_COLS):
        cols = slice(c0, c0 + MXU_COLS)
        heads = [(hc // HEAD_DIM_A, slice(hc, hc + HEAD_DIM_A), slice(hc - c0, hc - c0 + HEAD_DIM_A))
                 for hc in range(c0, c0 + MXU_COLS, HEAD_DIM_A)]
        u = proj(c0) * _sigmoid(proj(off_g + c0))
        v = proj(off_gc + c0) * proj(off_hb + c0)
        for hd, hcols, lcols in heads:
            u_s[hd] = u[:, lcols]
            v_s[hd] = v[:, lcols]
            conv_a(hd, hcols)
            causal_taps(sb_refs[hd], v_s, hd, hist_b, cbw_ref, hcols, cb_s)
            for q in range(nseq):
                nb_ref[q * hist_b:(q + 1) * hist_b, hcols] = v_s[hd, (q + 1) * steps - hist_b:(q + 1) * steps, :]
        ca = jnp.concatenate([ca_s[hd] for hd, _, _ in heads], axis=-1) + cab_ref[:, cols]
        ca = _head_layernorm(ca, lng_ref[:, cols], lnb_ref[:, cols])
        y_s[:, _mix_col(c0):_mix_col(c0) + MXU_COLS] = (_silu(ca) * _silu(proj(off_za + c0))).astype(BF16)
        cb = jnp.concatenate([cb_s[hd] for hd, _, _ in heads], axis=-1)
        y_s[:, _mix_col(c0) + MXU_COLS:_mix_col(c0) + 2 * MXU_COLS] = (
            proj(off_gb + c0) * cb * _silu(proj(off_zb + c0))).astype(BF16)

    out = x + _out_proj(y_s[...], wout_ref)
    xs_s[blk] = out
    is_last = layer == pl.num_programs(0) - 1
    y_ref[...] = jnp.where(is_last, _rms_scale(out) * fg_ref[...], out)


def _sample_trunk(x, state_a, state_b, norm_g, w_in, conv_a_w, conv_a_b, ln_a_g, ln_a_b, conv_b_w,
                  w_out, final_g):
    nseq, steps, d_model = x.shape
    depth, in_slices = w_in.shape[:2]
    conv_a_width, d_a = conv_a_w.shape[1:]
    conv_b_width, d_b = conv_b_w.shape[1:]
    out_slices, d_mix = w_out.shape[1:3]
    hist_a, hist_b = conv_a_width - 1, conv_b_width - 1
    n_heads = d_a // HEAD_DIM_A
    sb = SAMPLE_SEQS
    nblk = nseq // sb
    assert nseq % sb == 0 and (sb * hist_b) % SUBLANES == 0
    assert hist_b <= steps <= hist_a
    hist_a_pad = -(-hist_a // SUBLANES) * SUBLANES
    assert hist_a + SUBLANES <= SAMPLE_ROWS_A and hist_a <= SAMPLE_ROWS_A - SUBLANES <= hist_a + steps
    assert d_a % MXU_COLS == 0 and d_a == d_b and d_mix == d_a + d_b

    def per_layer(*block):
        return pl.BlockSpec((None,) + block, lambda l, i: (l,) + (0,) * len(block),
                            pipeline_mode=pl.Buffered(1))

    def head_slab_a(hd):
        return pl.BlockSpec((None, sb, hist_a_pad, HEAD_DIM_A), lambda l, i: (l, i, 0, hd))

    def head_slab_b(hd):
        return pl.BlockSpec((None, sb * hist_b, HEAD_DIM_A), lambda l, i: (l, i, hd))

    in_specs = (
        [pl.BlockSpec((sb * steps, d_model), lambda l, i: (i, 0))]
        + [head_slab_a(hd) for hd in range(n_heads)]
        + [head_slab_b(hd) for hd in range(n_heads)]
        + [per_layer(1, d_model),
           per_layer(in_slices, d_model, MXU_COLS),
           per_layer(conv_a_width, d_a), per_layer(1, d_a), per_layer(1, d_a), per_layer(1, d_a),
           per_layer(conv_b_width, d_b),
           per_layer(out_slices, d_mix, MXU_COLS),
           pl.BlockSpec((1, d_model), lambda l, i: (0, 0))])
    out_specs = [
        pl.BlockSpec((None, sb * steps, d_model), lambda l, i: (l, i, 0)),
        pl.BlockSpec((None, sb, hist_a_pad, d_a), lambda l, i: (l, i, 0, 0)),
        pl.BlockSpec((None, sb * hist_b, d_b), lambda l, i: (l, i, 0)),
    ]
    out_shape = [
        jax.ShapeDtypeStruct((depth, nseq * steps, d_model), F32),
        jax.ShapeDtypeStruct((depth, nseq, hist_a, d_a), F32),
        jax.ShapeDtypeStruct((depth, nseq * hist_b, d_b), F32),
    ]
    slab = lambda: pltpu.VMEM((n_heads, sb * steps, HEAD_DIM_A), F32)
    scratch = [
        pltpu.VMEM((nblk, sb * steps, d_model), F32),
        pltpu.VMEM((n_heads, sb, SAMPLE_ROWS_A, HEAD_DIM_A), F32),
        slab(), slab(), slab(), slab(),
        pltpu.VMEM((sb * steps, d_mix), BF16),
    ]
    sb2 = state_b.reshape(depth, nseq * hist_b, d_b)
    y, na, nb = pl.pallas_call(
        functools.partial(_sample_kernel, steps=steps, n_heads=n_heads),
        grid=(depth, nblk),
        in_specs=in_specs, out_specs=out_specs, out_shape=out_shape, scratch_shapes=scratch,
        compiler_params=pltpu.CompilerParams(
            dimension_semantics=("arbitrary", "arbitrary"), vmem_limit_bytes=VMEM_LIMIT_BYTES),
        name="sample_trunk",
    )(x.reshape(nseq * steps, d_model), *([state_a] * n_heads), *([sb2] * n_heads),
      norm_g[:, None, :], w_in, conv_a_w, conv_a_b[:, None, :], ln_a_g[:, None, :], ln_a_b[:, None, :],
      conv_b_w, w_out, final_g[None, :])
    return (y[depth - 1].reshape(nseq, steps, d_model), na,
            nb.reshape(depth, nseq, hist_b, d_b))


def _slice_cast_kernel(w_ref, o_ref, *, row_blocks):
    for j in range(o_ref.shape[0]):
        cols = slice(j * MXU_COLS, (j + 1) * MXU_COLS)
        r = 0
        for start, size in row_blocks:
            o_ref[j, r:r + size, :] = w_ref[start:start + size, cols].astype(BF16)
            r += size


def _column_slices(w, row_blocks=None):
    depth, rows, cols = w.shape
    row_blocks = row_blocks or ((0, rows),)
    group = WEIGHT_PREP_COLS // MXU_COLS
    return pl.pallas_call(
        functools.partial(_slice_cast_kernel, row_blocks=row_blocks),
        grid=(depth, cols // WEIGHT_PREP_COLS),
        in_specs=[pl.BlockSpec((None, rows, WEIGHT_PREP_COLS), lambda l, j: (l, 0, j))],
        out_specs=pl.BlockSpec((None, group, rows, MXU_COLS), lambda l, j: (l, j, 0, 0)),
        out_shape=jax.ShapeDtypeStruct((depth, cols // MXU_COLS, rows, MXU_COLS), BF16),
        compiler_params=pltpu.CompilerParams(
            dimension_semantics=("arbitrary", "arbitrary"), vmem_limit_bytes=VMEM_LIMIT_BYTES),
        name="weight_slices",
    )(w)


def _mix_row_blocks(d_a, d_mix):
    return tuple((g * d_a + c0, MXU_COLS) for c0 in range(0, d_a, MXU_COLS) for g in range(d_mix // d_a))


def kernel(x_prompt, x_sample, state_conv_a, state_conv_b, norm_g, w_in, conv_a_w, conv_a_b,
           ln_a_g, ln_a_b, conv_b_w, w_out, final_g):
    depth = w_in.shape[0]
    w_in_bf = _column_slices(w_in)
    w_out_bf = _column_slices(w_out, _mix_row_blocks(conv_a_w.shape[2], w_out.shape[1]))
    weights = (norm_g, w_in_bf, conv_a_w, conv_a_b, ln_a_g, ln_a_b, conv_b_w, w_out_bf, final_g)

    xp = x_prompt
    pa, pb = [], []
    for layer in range(depth):
        xp, na, nb = _prompt_layer(xp, layer, *weights, apply_final=layer == depth - 1)
        pa.append(na)
        pb.append(nb)
    ys, sa, sb = _sample_trunk(x_sample, state_conv_a, state_conv_b, *weights)
    return (xp, ys, jnp.stack(pa), jnp.stack(pb), sa, sb)
```

```python
import functools

import jax
import jax.numpy as jnp
from jax import lax
from jax.experimental import pallas as pl
from jax.experimental.pallas import tpu as pltpu

F32 = jnp.float32
BF16 = jnp.bfloat16

HEAD_DIM_A = 128
RMS_EPS = 1e-6
LN_EPS = 1e-5

SUBLANES = 8
MXU_COLS = 256
HALO_A = 32
HALO_B = 8
ROW_GROUP = 16
PROMPT_TILE = 512
SAMPLE_SEQS = 32
VMEM_LIMIT_BYTES = 56 * 1024 * 1024
WEIGHT_PREP_COLS = 1024


def _sigmoid(x):
    return 1.0 / (1.0 + jnp.exp(-x))


def _silu(x):
    return x * _sigmoid(x)


def _rms_scale(x):
    return x * lax.rsqrt(jnp.mean(x * x, axis=-1, keepdims=True) + RMS_EPS)


def _head_layernorm(ca, g, b):
    outs = []
    for j in range(ca.shape[-1] // HEAD_DIM_A):
        c = ca[:, j * HEAD_DIM_A:(j + 1) * HEAD_DIM_A]
        mu = jnp.mean(c, axis=-1, keepdims=True)
        d = c - mu
        var = jnp.mean(d * d, axis=-1, keepdims=True)
        outs.append(d * lax.rsqrt(var + LN_EPS))
    return jnp.concatenate(outs, axis=-1) * g + b


def _mix_col(c0):
    return 2 * c0


def _out_proj(y, wout_ref):
    return jnp.concatenate(
        [jnp.dot(y, wout_ref[n], preferred_element_type=F32) for n in range(wout_ref.shape[0])], axis=-1)


def _prompt_layer_kernel(x_ref, ng_ref, win_ref, caw_ref, cab_ref, lng_ref, lnb_ref, cbw_ref,
                         wout_ref, fg_ref, y_ref, na_ref, nb_ref,
                         h_s, u_s, v_s, p_s, y_s, *, apply_final):
    tm, d_model = x_ref.shape
    conv_a_w, d_a = caw_ref.shape
    conv_b_w, d_b = cbw_ref.shape
    off_g, off_za = d_a, 2 * d_a
    off_gb, off_gc, off_hb, off_zb = 3 * d_a, 3 * d_a + d_b, 3 * d_a + 2 * d_b, 3 * d_a + 3 * d_b
    hist_a = conv_a_w - 1
    hist_b = conv_b_w - 1

    n_heads = d_a // HEAD_DIM_A

    def head_slabs(c0, width):
        return [(hc // HEAD_DIM_A, slice(hc, hc + HEAD_DIM_A), slice(hc - c0, hc - c0 + HEAD_DIM_A))
                for hc in range(c0, c0 + width, HEAD_DIM_A)]

    @pl.when(pl.program_id(1) == 0)
    def _():
        u_s[:, 0:HALO_A, :] = jnp.zeros((n_heads, HALO_A, HEAD_DIM_A), F32)
        v_s[:, 0:HALO_B, :] = jnp.zeros((n_heads, HALO_B, HEAD_DIM_A), F32)

    x = x_ref[...]
    h_s[...] = (_rms_scale(x) * ng_ref[...]).astype(BF16)

    def proj(off):
        return jnp.dot(h_s[...], win_ref[off // MXU_COLS], preferred_element_type=F32)

    def project(c0, slot):
        u = proj(c0) * _sigmoid(proj(off_g + c0))
        v = proj(off_gc + c0) * proj(off_hb + c0)
        for hd, _, lcols in head_slabs(c0, MXU_COLS):
            u_s[hd, HALO_A:HALO_A + tm, :] = u[:, lcols]
            v_s[hd, HALO_B:HALO_B + tm, :] = v[:, lcols]
        p_s[slot, 0] = proj(off_za + c0)
        p_s[slot, 1] = proj(off_gb + c0)
        p_s[slot, 2] = proj(off_zb + c0)

    def mix(c0, slot):
        for hd, hcols, lcols in head_slabs(c0, MXU_COLS):
            taps = [jnp.broadcast_to(caw_ref[k:k + 1, hcols], (SUBLANES, HEAD_DIM_A))
                    for k in range(conv_a_w)]
            ycol = _mix_col(c0) + lcols.start
            for r in range(0, tm, ROW_GROUP):
                accs = []
                for s in range(r, r + ROW_GROUP, SUBLANES):
                    acc = None
                    for k in range(conv_a_w):
                        start = s + HALO_A - hist_a + k
                        term = u_s[hd, start:start + SUBLANES, :] * taps[k]
                        acc = term if acc is None else acc + term
                    accs.append(acc)
                ca = jnp.concatenate(accs, axis=0) + cab_ref[:, hcols]
                ca = _head_layernorm(ca, lng_ref[:, hcols], lnb_ref[:, hcols])
                y_s[r:r + ROW_GROUP, ycol:ycol + HEAD_DIM_A] = (
                    _silu(ca) * _silu(p_s[slot, 0, r:r + ROW_GROUP, lcols])).astype(BF16)
        cbs = []
        for hd, hcols, _ in head_slabs(c0, MXU_COLS):
            cb = None
            for k in range(conv_b_w):
                start = HALO_B - hist_b + k
                term = v_s[hd, start:start + tm, :] * cbw_ref[k:k + 1, hcols]
                cb = term if cb is None else cb + term
            cbs.append(cb)
        ycol = _mix_col(c0) + MXU_COLS
        y_s[:, ycol:ycol + MXU_COLS] = (
            p_s[slot, 1] * jnp.concatenate(cbs, axis=-1) * _silu(p_s[slot, 2])).astype(BF16)

    def project_out(c0):
        rows = slice(_mix_col(c0), _mix_col(c0) + 2 * MXU_COLS)
        for n in range(wout_ref.shape[0]):
            ncols = slice(n * MXU_COLS, (n + 1) * MXU_COLS)
            part = jnp.dot(y_s[:, rows], wout_ref[n, rows, :], preferred_element_type=F32)
            y_ref[:, ncols] = (x_ref[:, ncols] if c0 == 0 else y_ref[:, ncols]) + part

    blocks = list(range(0, d_a, MXU_COLS))
    project(blocks[0], 0)
    for i, c0 in enumerate(blocks):
        if i + 1 < len(blocks):
            project(blocks[i + 1], (i + 1) % 2)
        mix(c0, i % 2)
        project_out(c0)

    for hd, hcols, _ in head_slabs(0, d_a):
        na_ref[:, hcols] = u_s[hd, HALO_A + tm - hist_a:HALO_A + tm, :]
        nb_ref[:, hcols] = v_s[hd, HALO_B + tm - hist_b:HALO_B + tm, :]
    u_s[:, 0:HALO_A, :] = u_s[:, tm:tm + HALO_A, :]
    v_s[:, 0:HALO_B, :] = v_s[:, tm:tm + HALO_B, :]

    if apply_final:
        y_ref[...] = _rms_scale(y_ref[...]) * fg_ref[...]


def _const_spec(block_shape, index):
    return pl.BlockSpec(block_shape, lambda *_: index, pipeline_mode=pl.Buffered(1))


def _prompt_layer(x, layer, norm_g, w_in, conv_a_w, conv_a_b, ln_a_g, ln_a_b, conv_b_w, w_out,
                  final_g, *, apply_final):
    bsz, seq, d_model = x.shape
    in_slices = w_in.shape[1]
    conv_a_width, d_a = conv_a_w.shape[1:]
    conv_b_width, d_b = conv_b_w.shape[1:]
    out_slices, d_mix = w_out.shape[1:3]
    tm = PROMPT_TILE
    assert seq % tm == 0 and tm % ROW_GROUP == 0
    assert d_a % MXU_COLS == 0 and d_a == d_b and d_mix == d_a + d_b
    assert conv_a_width - 1 <= HALO_A and conv_b_width - 1 <= HALO_B

    lay = lambda *tail: (layer,) + tail
    in_specs = [
        pl.BlockSpec((None, tm, d_model), lambda b, t: (b, t, 0)),
        _const_spec((None, 1, d_model), lay(0, 0)),
        _const_spec((None, in_slices, d_model, MXU_COLS), lay(0, 0, 0)),
        _const_spec((None, conv_a_width, d_a), lay(0, 0)),
        _const_spec((None, 1, d_a), lay(0, 0)),
        _const_spec((None, 1, d_a), lay(0, 0)),
        _const_spec((None, 1, d_a), lay(0, 0)),
        _const_spec((None, conv_b_width, d_b), lay(0, 0)),
        _const_spec((None, out_slices, d_mix, MXU_COLS), lay(0, 0, 0)),
        _const_spec((1, d_model), (0, 0)),
    ]
    out_specs = [
        pl.BlockSpec((None, tm, d_model), lambda b, t: (b, t, 0)),
        pl.BlockSpec((None, conv_a_width - 1, d_a), lambda b, t: (b, 0, 0)),
        pl.BlockSpec((None, conv_b_width - 1, d_b), lambda b, t: (b, 0, 0)),
    ]
    out_shape = [
        jax.ShapeDtypeStruct((bsz, seq, d_model), F32),
        jax.ShapeDtypeStruct((bsz, conv_a_width - 1, d_a), F32),
        jax.ShapeDtypeStruct((bsz, conv_b_width - 1, d_b), F32),
    ]
    scratch = [
        pltpu.VMEM((tm, d_model), BF16),
        pltpu.VMEM((d_a // HEAD_DIM_A, HALO_A + tm, HEAD_DIM_A), F32),
        pltpu.VMEM((d_b // HEAD_DIM_A, HALO_B + tm, HEAD_DIM_A), F32),
        pltpu.VMEM((2, 3, tm, MXU_COLS), F32),
        pltpu.VMEM((tm, d_mix), BF16),
    ]
    return pl.pallas_call(
        functools.partial(_prompt_layer_kernel, apply_final=apply_final),
        grid=(bsz, seq // tm),
        in_specs=in_specs, out_specs=out_specs, out_shape=out_shape, scratch_shapes=scratch,
        compiler_params=pltpu.CompilerParams(
            dimension_semantics=("arbitrary", "arbitrary"), vmem_limit_bytes=VMEM_LIMIT_BYTES),
        name=f"prompt_layer{layer}",
    )(x, norm_g[:, None, :], w_in, conv_a_w, conv_a_b[:, None, :], ln_a_g[:, None, :],
      ln_a_b[:, None, :], conv_b_w, w_out, final_g[None, :])


def _sample_kernel(x_ref, sa_ref, sb_ref, ng_ref, win_ref, caw_ref, cab_ref, lng_ref, lnb_ref, cbw_ref,
                   wout_ref, fg_ref, y_ref, na_ref, nb_ref, xs_s, y_s):
    layer, blk = pl.program_id(0), pl.program_id(1)
    steps, nseq, d_model = x_ref.shape
    conv_a_w, d_a = caw_ref.shape
    conv_b_w, d_b = cbw_ref.shape
    off_g, off_za = d_a, 2 * d_a
    off_gb, off_gc, off_hb, off_zb = 3 * d_a, 3 * d_a + d_b, 3 * d_a + 2 * d_b, 3 * d_a + 3 * d_b

    @pl.when(layer == 0)
    def _():
        xs_s[blk] = jnp.concatenate([x_ref[t] for t in range(steps)], axis=0)

    x = xs_s[blk]
    h = (_rms_scale(x) * ng_ref[...]).astype(BF16)

    def proj(off):
        return jnp.dot(h, win_ref[off // MXU_COLS], preferred_element_type=F32)

    def step_rows(a, t):
        return a[t * nseq:(t + 1) * nseq]

    def causal_conv(hist_ref, new, w_ref, cols):
        hist = hist_ref.shape[0]
        acc = [None] * steps
        for j in range(hist + steps):
            row = hist_ref[j, :, cols] if j < hist else step_rows(new, j - hist)
            for t in range(steps):
                k = j - t
                if 0 <= k <= hist:
                    term = row * w_ref[k:k + 1, cols]
                    acc[t] = term if acc[t] is None else acc[t] + term
        return jnp.concatenate(acc, axis=0)

    def next_history(hist_ref, new, out_ref, cols):
        hist = hist_ref.shape[0]
        for j in range(hist):
            src = j + steps
            out_ref[j, :, cols] = hist_ref[src, :, cols] if src < hist else step_rows(new, src - hist)

    for c0 in range(0, d_a, MXU_COLS):
        cols = slice(c0, c0 + MXU_COLS)
        u = proj(c0) * _sigmoid(proj(off_g + c0))
        ca = causal_conv(sa_ref, u, caw_ref, cols) + cab_ref[:, cols]
        ca = _head_layernorm(ca, lng_ref[:, cols], lnb_ref[:, cols])
        y_s[:, _mix_col(c0):_mix_col(c0) + MXU_COLS] = (_silu(ca) * _silu(proj(off_za + c0))).astype(BF16)
        next_history(sa_ref, u, na_ref, cols)

        v = proj(off_gc + c0) * proj(off_hb + c0)
        cb = causal_conv(sb_ref, v, cbw_ref, cols)
        y_s[:, _mix_col(c0) + MXU_COLS:_mix_col(c0) + 2 * MXU_COLS] = (
            proj(off_gb + c0) * cb * _silu(proj(off_zb + c0))).astype(BF16)
        next_history(sb_ref, v, nb_ref, cols)

    out = x + _out_proj(y_s[...], wout_ref)
    xs_s[blk] = out
    is_last = layer == pl.num_programs(0) - 1
    y = jnp.where(is_last, _rms_scale(out) * fg_ref[...], out)
    for t in range(steps):
        y_ref[t] = step_rows(y, t)


def _sample_trunk(x, state_a, state_b, norm_g, w_in, conv_a_w, conv_a_b, ln_a_g, ln_a_b, conv_b_w,
                  w_out, final_g):
    nseq, steps, d_model = x.shape
    depth, in_slices = w_in.shape[:2]
    conv_a_width, d_a = conv_a_w.shape[1:]
    conv_b_width, d_b = conv_b_w.shape[1:]
    out_slices, d_mix = w_out.shape[1:3]
    hist_a, hist_b = conv_a_width - 1, conv_b_width - 1
    sb = SAMPLE_SEQS
    nblk = nseq // sb
    assert nseq % sb == 0 and sb % SUBLANES == 0
    assert d_a % MXU_COLS == 0 and d_a == d_b and d_mix == d_a + d_b

    def per_layer(*block):
        return pl.BlockSpec((None,) + block, lambda l, i: (l,) + (0,) * len(block),
                            pipeline_mode=pl.Buffered(1))

    def history(hist, width):
        return pl.BlockSpec((None, hist, sb, width), lambda l, i: (l, 0, i, 0))

    in_specs = [
        pl.BlockSpec((steps, sb, d_model), lambda l, i: (0, i, 0)),
        history(hist_a, d_a), history(hist_b, d_b),
        per_layer(1, d_model),
        per_layer(in_slices, d_model, MXU_COLS),
        per_layer(conv_a_width, d_a), per_layer(1, d_a), per_layer(1, d_a), per_layer(1, d_a),
        per_layer(conv_b_width, d_b),
        per_layer(out_slices, d_mix, MXU_COLS),
        pl.BlockSpec((1, d_model), lambda l, i: (0, 0)),
    ]
    out_specs = [
        pl.BlockSpec((None, steps, sb, d_model), lambda l, i: (l, 0, i, 0)),
        history(hist_a, d_a), history(hist_b, d_b),
    ]
    out_shape = [
        jax.ShapeDtypeStruct((depth, steps, nseq, d_model), F32),
        jax.ShapeDtypeStruct((depth, hist_a, nseq, d_a), F32),
        jax.ShapeDtypeStruct((depth, hist_b, nseq, d_b), F32),
    ]
    scratch = [
        pltpu.VMEM((nblk, sb * steps, d_model), F32),
        pltpu.VMEM((sb * steps, d_mix), BF16),
    ]
    y, na, nb = pl.pallas_call(
        _sample_kernel,
        grid=(depth, nblk),
        in_specs=in_specs, out_specs=out_specs, out_shape=out_shape, scratch_shapes=scratch,
        compiler_params=pltpu.CompilerParams(
            dimension_semantics=("arbitrary", "arbitrary"), vmem_limit_bytes=VMEM_LIMIT_BYTES),
        name="sample_trunk",
    )(x.transpose(1, 0, 2), state_a.transpose(0, 2, 1, 3), state_b.transpose(0, 2, 1, 3),
      norm_g[:, None, :], w_in, conv_a_w, conv_a_b[:, None, :], ln_a_g[:, None, :], ln_a_b[:, None, :],
      conv_b_w, w_out, final_g[None, :])
    return y[depth - 1].transpose(1, 0, 2), na.transpose(0, 2, 1, 3), nb.transpose(0, 2, 1, 3)


def _slice_cast_kernel(w_ref, o_ref, *, row_blocks):
    for j in range(o_ref.shape[0]):
        cols = slice(j * MXU_COLS, (j + 1) * MXU_COLS)
        r = 0
        for start, size in row_blocks:
            o_ref[j, r:r + size, :] = w_ref[start:start + size, cols].astype(BF16)
            r += size


def _column_slices(w, row_blocks=None):
    depth, rows, cols = w.shape
    row_blocks = row_blocks or ((0, rows),)
    group = WEIGHT_PREP_COLS // MXU_COLS
    return pl.pallas_call(
        functools.partial(_slice_cast_kernel, row_blocks=row_blocks),
        grid=(depth, cols // WEIGHT_PREP_COLS),
        in_specs=[pl.BlockSpec((None, rows, WEIGHT_PREP_COLS), lambda l, j: (l, 0, j))],
        out_specs=pl.BlockSpec((None, group, rows, MXU_COLS), lambda l, j: (l, j, 0, 0)),
        out_shape=jax.ShapeDtypeStruct((depth, cols // MXU_COLS, rows, MXU_COLS), BF16),
        compiler_params=pltpu.CompilerParams(
            dimension_semantics=("arbitrary", "arbitrary"), vmem_limit_bytes=VMEM_LIMIT_BYTES),
        name="weight_slices",
    )(w)


def _mix_row_blocks(d_a, d_mix):
    return tuple((g * d_a + c0, MXU_COLS) for c0 in range(0, d_a, MXU_COLS) for g in range(d_mix // d_a))


def kernel(x_prompt, x_sample, state_conv_a, state_conv_b, norm_g, w_in, conv_a_w, conv_a_b,
           ln_a_g, ln_a_b, conv_b_w, w_out, final_g):
    depth = w_in.shape[0]
    w_in_bf = _column_slices(w_in)
    w_out_bf = _column_slices(w_out, _mix_row_blocks(conv_a_w.shape[2], w_out.shape[1]))
    weights = (norm_g, w_in_bf, conv_a_w, conv_a_b, ln_a_g, ln_a_b, conv_b_w, w_out_bf, final_g)

    xp = x_prompt
    pa, pb = [], []
    for layer in range(depth):
        xp, na, nb = _prompt_layer(xp, layer, *weights, apply_final=layer == depth - 1)
        pa.append(na)
        pb.append(nb)
    ys, sa, sb = _sample_trunk(x_sample, state_conv_a, state_conv_b, *weights)
    return (xp, ys, jnp.stack(pa), jnp.stack(pb), sa, sb)
```

```python
import functools

import jax
import jax.numpy as jnp
from jax import lax
from jax.experimental import pallas as pl
from jax.experimental.pallas import tpu as pltpu

F32 = jnp.float32
BF16 = jnp.bfloat16

HEAD_DIM_A = 128
RMS_EPS = 1e-6
LN_EPS = 1e-5

SUBLANES = 8
MXU_COLS = 256
HALO_A = 32
HALO_B = 8
ROW_GROUP = 16
CONV_BF16_TAPS = 4
PROMPT_TILE = 512
SAMPLE_SEQS = 32
VMEM_LIMIT_BYTES = 56 * 1024 * 1024
WEIGHT_PREP_COLS = 1024


def _sigmoid(x):
    return 1.0 / (1.0 + jnp.exp(-x))


def _silu(x):
    return x * _sigmoid(x)


def _rms_scale(x):
    return x * lax.rsqrt(jnp.mean(x * x, axis=-1, keepdims=True) + RMS_EPS)


def _head_layernorm(ca, g, b):
    outs = []
    for j in range(ca.shape[-1] // HEAD_DIM_A):
        c = ca[:, j * HEAD_DIM_A:(j + 1) * HEAD_DIM_A]
        mu = jnp.mean(c, axis=-1, keepdims=True)
        d = c - mu
        var = jnp.mean(d * d, axis=-1, keepdims=True)
        outs.append(d * lax.rsqrt(var + LN_EPS))
    return jnp.concatenate(outs, axis=-1) * g + b


def _mix_col(c0):
    return 2 * c0


def _out_proj(y, wout_ref):
    return jnp.concatenate(
        [jnp.dot(y, wout_ref[n], preferred_element_type=F32) for n in range(wout_ref.shape[0])], axis=-1)


def _prompt_layer_kernel(x_ref, ng_ref, win_ref, caw_ref, cab_ref, lng_ref, lnb_ref, cbw_ref,
                         wout_ref, fg_ref, y_ref, na_ref, nb_ref,
                         h_s, u_s, q_s, v_s, p_s, y_s, *, apply_final):
    tm, d_model = x_ref.shape
    conv_a_w, d_a = caw_ref.shape
    conv_b_w, d_b = cbw_ref.shape
    off_g, off_za = d_a, 2 * d_a
    off_gb, off_gc, off_hb, off_zb = 3 * d_a, 3 * d_a + d_b, 3 * d_a + 2 * d_b, 3 * d_a + 3 * d_b
    hist_a = conv_a_w - 1
    hist_b = conv_b_w - 1

    n_heads = d_a // HEAD_DIM_A

    def head_slabs(c0, width):
        return [(hc // HEAD_DIM_A, slice(hc, hc + HEAD_DIM_A), slice(hc - c0, hc - c0 + HEAD_DIM_A))
                for hc in range(c0, c0 + width, HEAD_DIM_A)]

    @pl.when(pl.program_id(1) == 0)
    def _():
        u_s[:, 0:HALO_A, :] = jnp.zeros((n_heads, HALO_A, HEAD_DIM_A), F32)
        v_s[:, 0:HALO_B, :] = jnp.zeros((n_heads, HALO_B, HEAD_DIM_A), F32)

    x = x_ref[...]
    h_s[...] = (_rms_scale(x) * ng_ref[...]).astype(BF16)

    def proj(off):
        return jnp.dot(h_s[...], win_ref[off // MXU_COLS], preferred_element_type=F32)

    def project(c0, slot):
        u = proj(c0) * _sigmoid(proj(off_g + c0))
        v = proj(off_gc + c0) * proj(off_hb + c0)
        for hd, _, lcols in head_slabs(c0, MXU_COLS):
            u_s[hd, HALO_A:HALO_A + tm, :] = u[:, lcols]
            v_s[hd, HALO_B:HALO_B + tm, :] = v[:, lcols]
            for r in range(0, HALO_A + tm - SUBLANES, SUBLANES):
                q_s[hd, r:r + SUBLANES, :] = pltpu.pack_elementwise(
                    [u_s[hd, r:r + SUBLANES, :], u_s[hd, r + SUBLANES:r + 2 * SUBLANES, :]], packed_dtype=BF16)
        p_s[slot, 0] = proj(off_za + c0)
        p_s[slot, 1] = proj(off_gb + c0)
        p_s[slot, 2] = proj(off_zb + c0)

    def mix(c0, slot):
        for hd, hcols, lcols in head_slabs(c0, MXU_COLS):
            taps = []
            for k in range(conv_a_w):
                w8 = jnp.broadcast_to(caw_ref[k:k + 1, hcols], (SUBLANES, HEAD_DIM_A))
                taps.append(pltpu.bitcast(pltpu.pack_elementwise([w8, w8], packed_dtype=BF16), BF16))
            ycol = _mix_col(c0) + lcols.start
            for r in range(0, tm, ROW_GROUP):
                lo = hi = None
                for k0 in range(0, conv_a_w, CONV_BF16_TAPS):
                    acc = None
                    for k in range(k0, min(k0 + CONV_BF16_TAPS, conv_a_w)):
                        start = r + HALO_A - hist_a + k
                        term = pltpu.bitcast(q_s[hd, start:start + SUBLANES, :], BF16) * taps[k]
                        acc = term if acc is None else acc + term
                    words = pltpu.bitcast(acc, jnp.uint32)
                    part_lo = pltpu.unpack_elementwise(words, index=0, packed_dtype=BF16, unpacked_dtype=F32)
                    part_hi = pltpu.unpack_elementwise(words, index=1, packed_dtype=BF16, unpacked_dtype=F32)
                    lo = part_lo if lo is None else lo + part_lo
                    hi = part_hi if hi is None else hi + part_hi
                ca = jnp.concatenate([lo, hi], axis=0) + cab_ref[:, hcols]
                ca = _head_layernorm(ca, lng_ref[:, hcols], lnb_ref[:, hcols])
                y_s[r:r + ROW_GROUP, ycol:ycol + HEAD_DIM_A] = (
                    _silu(ca) * _silu(p_s[slot, 0, r:r + ROW_GROUP, lcols])).astype(BF16)
        cbs = []
        for hd, hcols, _ in head_slabs(c0, MXU_COLS):
            cb = None
            for k in range(conv_b_w):
                start = HALO_B - hist_b + k
                term = v_s[hd, start:start + tm, :] * cbw_ref[k:k + 1, hcols]
                cb = term if cb is None else cb + term
            cbs.append(cb)
        ycol = _mix_col(c0) + MXU_COLS
        y_s[:, ycol:ycol + MXU_COLS] = (
            p_s[slot, 1] * jnp.concatenate(cbs, axis=-1) * _silu(p_s[slot, 2])).astype(BF16)

    def project_out(c0):
        rows = slice(_mix_col(c0), _mix_col(c0) + 2 * MXU_COLS)
        for n in range(wout_ref.shape[0]):
            ncols = slice(n * MXU_COLS, (n + 1) * MXU_COLS)
            part = jnp.dot(y_s[:, rows], wout_ref[n, rows, :], preferred_element_type=F32)
            y_ref[:, ncols] = (x_ref[:, ncols] if c0 == 0 else y_ref[:, ncols]) + part

    blocks = list(range(0, d_a, MXU_COLS))
    project(blocks[0], 0)
    for i, c0 in enumerate(blocks):
        if i + 1 < len(blocks):
            project(blocks[i + 1], (i + 1) % 2)
        mix(c0, i % 2)
        project_out(c0)

    for hd, hcols, _ in head_slabs(0, d_a):
        na_ref[:, hcols] = u_s[hd, HALO_A + tm - hist_a:HALO_A + tm, :]
        nb_ref[:, hcols] = v_s[hd, HALO_B + tm - hist_b:HALO_B + tm, :]
    u_s[:, 0:HALO_A, :] = u_s[:, tm:tm + HALO_A, :]
    v_s[:, 0:HALO_B, :] = v_s[:, tm:tm + HALO_B, :]

    if apply_final:
        y_ref[...] = _rms_scale(y_ref[...]) * fg_ref[...]


def _const_spec(block_shape, index):
    return pl.BlockSpec(block_shape, lambda *_: index, pipeline_mode=pl.Buffered(1))


def _prompt_layer(x, layer, norm_g, w_in, conv_a_w, conv_a_b, ln_a_g, ln_a_b, conv_b_w, w_out,
                  final_g, *, apply_final):
    bsz, seq, d_model = x.shape
    in_slices = w_in.shape[1]
    conv_a_width, d_a = conv_a_w.shape[1:]
    conv_b_width, d_b = conv_b_w.shape[1:]
    out_slices, d_mix = w_out.shape[1:3]
    tm = PROMPT_TILE
    assert seq % tm == 0 and tm % ROW_GROUP == 0
    assert d_a % MXU_COLS == 0 and d_a == d_b and d_mix == d_a + d_b
    assert conv_a_width - 1 <= HALO_A and conv_b_width - 1 <= HALO_B

    lay = lambda *tail: (layer,) + tail
    in_specs = [
        pl.BlockSpec((None, tm, d_model), lambda b, t: (b, t, 0)),
        _const_spec((None, 1, d_model), lay(0, 0)),
        _const_spec((None, in_slices, d_model, MXU_COLS), lay(0, 0, 0)),
        _const_spec((None, conv_a_width, d_a), lay(0, 0)),
        _const_spec((None, 1, d_a), lay(0, 0)),
        _const_spec((None, 1, d_a), lay(0, 0)),
        _const_spec((None, 1, d_a), lay(0, 0)),
        _const_spec((None, conv_b_width, d_b), lay(0, 0)),
        _const_spec((None, out_slices, d_mix, MXU_COLS), lay(0, 0, 0)),
        _const_spec((1, d_model), (0, 0)),
    ]
    out_specs = [
        pl.BlockSpec((None, tm, d_model), lambda b, t: (b, t, 0)),
        pl.BlockSpec((None, conv_a_width - 1, d_a), lambda b, t: (b, 0, 0)),
        pl.BlockSpec((None, conv_b_width - 1, d_b), lambda b, t: (b, 0, 0)),
    ]
    out_shape = [
        jax.ShapeDtypeStruct((bsz, seq, d_model), F32),
        jax.ShapeDtypeStruct((bsz, conv_a_width - 1, d_a), F32),
        jax.ShapeDtypeStruct((bsz, conv_b_width - 1, d_b), F32),
    ]
    scratch = [
        pltpu.VMEM((tm, d_model), BF16),
        pltpu.VMEM((d_a // HEAD_DIM_A, HALO_A + tm, HEAD_DIM_A), F32),
        pltpu.VMEM((d_a // HEAD_DIM_A, HALO_A + tm, HEAD_DIM_A), jnp.uint32),
        pltpu.VMEM((d_b // HEAD_DIM_A, HALO_B + tm, HEAD_DIM_A), F32),
        pltpu.VMEM((2, 3, tm, MXU_COLS), F32),
        pltpu.VMEM((tm, d_mix), BF16),
    ]
    return pl.pallas_call(
        functools.partial(_prompt_layer_kernel, apply_final=apply_final),
        grid=(bsz, seq // tm),
        in_specs=in_specs, out_specs=out_specs, out_shape=out_shape, scratch_shapes=scratch,
        compiler_params=pltpu.CompilerParams(
            dimension_semantics=("arbitrary", "arbitrary"), vmem_limit_bytes=VMEM_LIMIT_BYTES),
        name=f"prompt_layer{layer}",
    )(x, norm_g[:, None, :], w_in, conv_a_w, conv_a_b[:, None, :], ln_a_g[:, None, :],
      ln_a_b[:, None, :], conv_b_w, w_out, final_g[None, :])


def _sample_kernel(x_ref, sa_ref, sb_ref, ng_ref, win_ref, caw_ref, cab_ref, lng_ref, lnb_ref, cbw_ref,
                   wout_ref, fg_ref, y_ref, na_ref, nb_ref, xs_s, y_s):
    layer, blk = pl.program_id(0), pl.program_id(1)
    steps, nseq, d_model = x_ref.shape
    conv_a_w, d_a = caw_ref.shape
    conv_b_w, d_b = cbw_ref.shape
    off_g, off_za = d_a, 2 * d_a
    off_gb, off_gc, off_hb, off_zb = 3 * d_a, 3 * d_a + d_b, 3 * d_a + 2 * d_b, 3 * d_a + 3 * d_b

    @pl.when(layer == 0)
    def _():
        xs_s[blk] = jnp.concatenate([x_ref[t] for t in range(steps)], axis=0)

    x = xs_s[blk]
    h = (_rms_scale(x) * ng_ref[...]).astype(BF16)

    def proj(off):
        return jnp.dot(h, win_ref[off // MXU_COLS], preferred_element_type=F32)

    def step_rows(a, t):
        return a[t * nseq:(t + 1) * nseq]

    def causal_conv(hist_ref, new, w_ref, cols):
        hist = hist_ref.shape[0]
        acc = [None] * steps
        for j in range(hist + steps):
            row = hist_ref[j, :, cols] if j < hist else step_rows(new, j - hist)
            for t in range(steps):
                k = j - t
                if 0 <= k <= hist:
                    term = row * w_ref[k:k + 1, cols]
                    acc[t] = term if acc[t] is None else acc[t] + term
        return jnp.concatenate(acc, axis=0)

    def next_history(hist_ref, new, out_ref, cols):
        hist = hist_ref.shape[0]
        for j in range(hist):
            src = j + steps
            out_ref[j, :, cols] = hist_ref[src, :, cols] if src < hist else step_rows(new, src - hist)

    for c0 in range(0, d_a, MXU_COLS):
        cols = slice(c0, c0 + MXU_COLS)
        u = proj(c0) * _sigmoid(proj(off_g + c0))
        ca = causal_conv(sa_ref, u, caw_ref, cols) + cab_ref[:, cols]
        ca = _head_layernorm(ca, lng_ref[:, cols], lnb_ref[:, cols])
        y_s[:, _mix_col(c0):_mix_col(c0) + MXU_COLS] = (_silu(ca) * _silu(proj(off_za + c0))).astype(BF16)
        next_history(sa_ref, u, na_ref, cols)

        v = proj(off_gc + c0) * proj(off_hb + c0)
        cb = causal_conv(sb_ref, v, cbw_ref, cols)
        y_s[:, _mix_col(c0) + MXU_COLS:_mix_col(c0) + 2 * MXU_COLS] = (
            proj(off_gb + c0) * cb * _silu(proj(off_zb + c0))).astype(BF16)
        next_history(sb_ref, v, nb_ref, cols)

    out = x + _out_proj(y_s[...], wout_ref)
    xs_s[blk] = out
    is_last = layer == pl.num_programs(0) - 1
    y = jnp.where(is_last, _rms_scale(out) * fg_ref[...], out)
    for t in range(steps):
        y_ref[t] = step_rows(y, t)


def _sample_trunk(x, state_a, state_b, norm_g, w_in, conv_a_w, conv_a_b, ln_a_g, ln_a_b, conv_b_w,
                  w_out, final_g):
    nseq, steps, d_model = x.shape
    depth, in_slices = w_in.shape[:2]
    conv_a_width, d_a = conv_a_w.shape[1:]
    conv_b_width, d_b = conv_b_w.shape[1:]
    out_slices, d_mix = w_out.shape[1:3]
    hist_a, hist_b = conv_a_width - 1, conv_b_width - 1
    sb = SAMPLE_SEQS
    nblk = nseq // sb
    assert nseq % sb == 0 and sb % SUBLANES == 0
    assert d_a % MXU_COLS == 0 and d_a == d_b and d_mix == d_a + d_b

    def per_layer(*block):
        return pl.BlockSpec((None,) + block, lambda l, i: (l,) + (0,) * len(block),
                            pipeline_mode=pl.Buffered(1))

    def history(hist, width):
        return pl.BlockSpec((None, hist, sb, width), lambda l, i: (l, 0, i, 0))

    in_specs = [
        pl.BlockSpec((steps, sb, d_model), lambda l, i: (0, i, 0)),
        history(hist_a, d_a), history(hist_b, d_b),
        per_layer(1, d_model),
        per_layer(in_slices, d_model, MXU_COLS),
        per_layer(conv_a_width, d_a), per_layer(1, d_a), per_layer(1, d_a), per_layer(1, d_a),
        per_layer(conv_b_width, d_b),
        per_layer(out_slices, d_mix, MXU_COLS),
        pl.BlockSpec((1, d_model), lambda l, i: (0, 0)),
    ]
    out_specs = [
        pl.BlockSpec((None, steps, sb, d_model), lambda l, i: (l, 0, i, 0)),
        history(hist_a, d_a), history(hist_b, d_b),
    ]
    out_shape = [
        jax.ShapeDtypeStruct((depth, steps, nseq, d_model), F32),
        jax.ShapeDtypeStruct((depth, hist_a, nseq, d_a), F32),
        jax.ShapeDtypeStruct((depth, hist_b, nseq, d_b), F32),
    ]
    scratch = [
        pltpu.VMEM((nblk, sb * steps, d_model), F32),
        pltpu.VMEM((sb * steps, d_mix), BF16),
    ]
    y, na, nb = pl.pallas_call(
        _sample_kernel,
        grid=(depth, nblk),
        in_specs=in_specs, out_specs=out_specs, out_shape=out_shape, scratch_shapes=scratch,
        compiler_params=pltpu.CompilerParams(
            dimension_semantics=("arbitrary", "arbitrary"), vmem_limit_bytes=VMEM_LIMIT_BYTES),
        name="sample_trunk",
    )(x.transpose(1, 0, 2), state_a.transpose(0, 2, 1, 3), state_b.transpose(0, 2, 1, 3),
      norm_g[:, None, :], w_in, conv_a_w, conv_a_b[:, None, :], ln_a_g[:, None, :], ln_a_b[:, None, :],
      conv_b_w, w_out, final_g[None, :])
    return y[depth - 1].transpose(1, 0, 2), na.transpose(0, 2, 1, 3), nb.transpose(0, 2, 1, 3)


def _slice_cast_kernel(w_ref, o_ref, *, row_blocks):
    for j in range(o_ref.shape[0]):
        cols = slice(j * MXU_COLS, (j + 1) * MXU_COLS)
        r = 0
        for start, size in row_blocks:
            o_ref[j, r:r + size, :] = w_ref[start:start + size, cols].astype(BF16)
            r += size


def _column_slices(w, row_blocks=None):
    depth, rows, cols = w.shape
    row_blocks = row_blocks or ((0, rows),)
    group = WEIGHT_PREP_COLS // MXU_COLS
    return pl.pallas_call(
        functools.partial(_slice_cast_kernel, row_blocks=row_blocks),
        grid=(depth, cols // WEIGHT_PREP_COLS),
        in_specs=[pl.BlockSpec((None, rows, WEIGHT_PREP_COLS), lambda l, j: (l, 0, j))],
        out_specs=pl.BlockSpec((None, group, rows, MXU_COLS), lambda l, j: (l, j, 0, 0)),
        out_shape=jax.ShapeDtypeStruct((depth, cols // MXU_COLS, rows, MXU_COLS), BF16),
        compiler_params=pltpu.CompilerParams(
            dimension_semantics=("arbitrary", "arbitrary"), vmem_limit_bytes=VMEM_LIMIT_BYTES),
        name="weight_slices",
    )(w)


def _mix_row_blocks(d_a, d_mix):
    return tuple((g * d_a + c0, MXU_COLS) for c0 in range(0, d_a, MXU_COLS) for g in range(d_mix // d_a))


def kernel(x_prompt, x_sample, state_conv_a, state_conv_b, norm_g, w_in, conv_a_w, conv_a_b,
           ln_a_g, ln_a_b, conv_b_w, w_out, final_g):
    depth = w_in.shape[0]
    w_in_bf = _column_slices(w_in)
    w_out_bf = _column_slices(w_out, _mix_row_blocks(conv_a_w.shape[2], w_out.shape[1]))
    weights = (norm_g, w_in_bf, conv_a_w, conv_a_b, ln_a_g, ln_a_b, conv_b_w, w_out_bf, final_g)

    xp = x_prompt
    pa, pb = [], []
    for layer in range(depth):
        xp, na, nb = _prompt_layer(xp, layer, *weights, apply_final=layer == depth - 1)
        pa.append(na)
        pb.append(nb)
    ys, sa, sb = _sample_trunk(x_sample, state_conv_a, state_conv_b, *weights)
    return (xp, ys, jnp.stack(pa), jnp.stack(pb), sa, sb)
```
